```python
import math
import jax, jax.numpy as jnp
from jax import lax
import numpy as np

D_MODEL = 1024
BATCH = 16
SEQ = 2048
DEPTH = 1

MIX_WIDTH = D_MODEL
DIFF_HEADS = 4
DIFF_HEAD_DIM = 64
DIFF_V_DIM = 2 * DIFF_HEAD_DIM
DIFF_WIDTH = DIFF_HEADS * DIFF_V_DIM
MLA_HEADS = 4
MLA_NOPE_DIM = 64
MLA_ROPE_DIM = 32
MLA_QK_DIM = MLA_NOPE_DIM + MLA_ROPE_DIM
MLA_V_DIM = (MIX_WIDTH - DIFF_WIDTH) // MLA_HEADS
MLA_WIDTH = MLA_HEADS * MLA_V_DIM
MLA_Q_RANK = 384
MLA_KV_RANK = 256
DIFF_QK_COLS = DIFF_HEADS * 2 * DIFF_HEAD_DIM
DIFF_V_COLS = DIFF_WIDTH
IN_SPLITS = (DIFF_QK_COLS,
             2 * DIFF_QK_COLS,
             2 * DIFF_QK_COLS + DIFF_V_COLS,
             2 * DIFF_QK_COLS + DIFF_V_COLS + MLA_Q_RANK,
             2 * DIFF_QK_COLS + DIFF_V_COLS + MLA_Q_RANK + MLA_KV_RANK)
IN_COLS = 2 * DIFF_QK_COLS + DIFF_V_COLS + MLA_Q_RANK + MLA_KV_RANK + MLA_ROPE_DIM
D_FF = 2816
ROPE_THETA = 10000.0
NORM_EPS = 1e-6
Q_BLOCK = 128
N_MOD = 9

kernel_name = "hymba_diffattn_mla_macaron_adaln"


def lambda_init_fn(layer_idx):
    return 0.8 - 0.6 * math.exp(-0.3 * layer_idx)


def rmsnorm(x, g):
    x32 = x.astype(jnp.float32)
    y = x32 * lax.rsqrt(jnp.mean(x32 * x32, axis=-1, keepdims=True) + NORM_EPS)
    return y.astype(x.dtype) * g


def modulate(xn, shift, scale):
    return xn * (1 + scale) + shift


def swiglu(x, w_gate, w_up, w_down):
    return (jax.nn.silu(x @ w_gate) * (x @ w_up)) @ w_down


def rope_tables(positions, dim):
    inv_freq = ROPE_THETA ** (-jnp.arange(0, dim, 2, dtype=jnp.float32) / dim)
    ang = positions.astype(jnp.float32)[..., None] * inv_freq
    return jnp.cos(ang), jnp.sin(ang)


def apply_rope(x, cos, sin):
    shape = cos.shape[:2] + (1,) * (x.ndim - 3) + cos.shape[-1:]
    cos = cos.reshape(shape).astype(x.dtype)
    sin = sin.reshape(shape).astype(x.dtype)
    x1, x2 = jnp.split(x, 2, axis=-1)
    return jnp.concatenate([x1 * cos - x2 * sin, x2 * cos + x1 * sin], axis=-1)


def causal_block_attention(q, k, v, scale, combine):
    S = q.shape[1]
    outs = []
    for i in range(S // Q_BLOCK):
        q0 = i * Q_BLOCK
        kend = q0 + Q_BLOCK
        qb = q[:, q0:kend]
        kb = k[:, :kend]
        vb = v[:, :kend]
        s = jnp.einsum('bqhmd,bkhmd->bhmqk', qb, kb,
                       preferred_element_type=jnp.float32) * scale
        causal = (q0 + jnp.arange(Q_BLOCK))[:, None] >= jnp.arange(kend)[None, :]
        s = jnp.where(causal, s, jnp.finfo(jnp.float32).min)
        p = jax.nn.softmax(s, axis=-1)
        w = combine(p).astype(v.dtype)
        outs.append(jnp.einsum('bhqk,bkhd->bqhd', w, vb))
    return jnp.concatenate(outs, axis=1)


def hybrid_mixer(h, cos_d, sin_d, cos_r, sin_r, layer_idx,
                 w_in, lq1, lk1, lq2, lk2, diff_subln,
                 mla_q_norm, mla_w_uq, mla_kv_norm, mla_w_ukv, mla_out_norm, w_out):
    B, S, _ = h.shape
    proj = h @ w_in
    dq, dk, dv, cq, ckv, kr = jnp.split(proj, IN_SPLITS, axis=-1)

    dq = apply_rope(dq.reshape(B, S, DIFF_HEADS, 2, DIFF_HEAD_DIM), cos_d, sin_d)
    dk = apply_rope(dk.reshape(B, S, DIFF_HEADS, 2, DIFF_HEAD_DIM), cos_d, sin_d)
    dv = dv.reshape(B, S, DIFF_HEADS, DIFF_V_DIM)
    lam_init = lambda_init_fn(layer_idx)
    f32 = jnp.float32
    lam = (jnp.exp(jnp.sum(lq1.astype(f32) * lk1.astype(f32)))
           - jnp.exp(jnp.sum(lq2.astype(f32) * lk2.astype(f32))) + lam_init)
    od = causal_block_attention(dq, dk, dv, DIFF_HEAD_DIM ** -0.5,
                                lambda p: p[:, :, 0] - lam * p[:, :, 1])
    od = rmsnorm(od, diff_subln) * (1 - lam_init)
    od = od.reshape(B, S, DIFF_WIDTH)

    cq = rmsnorm(cq, mla_q_norm)
    q = (cq @ mla_w_uq).reshape(B, S, MLA_HEADS, MLA_QK_DIM)
    q_nope, q_rope = jnp.split(q, [MLA_NOPE_DIM], axis=-1)
    q_rope = apply_rope(q_rope, cos_r, sin_r)
    ckv = rmsnorm(ckv, mla_kv_norm)
    kv = (ckv @ mla_w_ukv).reshape(B, S, MLA_HEADS, MLA_NOPE_DIM + MLA_V_DIM)
    k_nope, mv = jnp.split(kv, [MLA_NOPE_DIM], axis=-1)
    kr = apply_rope(kr, cos_r, sin_r)
    kr = jnp.broadcast_to(kr[:, :, None, :], (B, S, MLA_HEADS, MLA_ROPE_DIM))
    qm = jnp.concatenate([q_nope, q_rope], axis=-1)[:, :, :, None, :]
    km = jnp.concatenate([k_nope, kr], axis=-1)[:, :, :, None, :]
    om = causal_block_attention(qm, km, mv, MLA_QK_DIM ** -0.5, lambda p: p[:, :, 0])
    om = rmsnorm(om.reshape(B, S, MLA_WIDTH), mla_out_norm)

    return jnp.concatenate([od, om], axis=-1) @ w_out


def setup_inputs(seed: int = 0) -> dict:
    key = jax.random.key(seed)
    ks = iter(jax.random.split(key, 40))
    nrm = lambda shape, s: jax.random.normal(next(ks), shape, jnp.float32) * s
    gain = lambda shape: 1.0 + nrm(shape, 0.02)
    D, L = D_MODEL, DEPTH
    offs = jax.random.randint(next(ks), (BATCH, 1), 0, 1024, dtype=jnp.int32)
    positions = offs + jnp.arange(SEQ, dtype=jnp.int32)[None, :]
    return {
        "x": nrm((BATCH, SEQ, D), 1.0),
        "c": nrm((BATCH, D), 1.0),
        "positions": positions,
        "w_ada": nrm((L, D, N_MOD * D), D ** -0.5),
        "b_ada": nrm((L, N_MOD * D), 0.02),
        "ffn1_norm": gain((L, D)),
        "ffn1_w_gate": nrm((L, D, D_FF), D ** -0.5),
        "ffn1_w_up": nrm((L, D, D_FF), D ** -0.5),
        "ffn1_w_down": nrm((L, D_FF, D), D_FF ** -0.5),
        "mix_norm": gain((L, D)),
        "w_in": nrm((L, D, IN_COLS), D ** -0.5),
        "diff_lambda_q1": nrm((L, DIFF_HEAD_DIM), 0.1),
        "diff_lambda_k1": nrm((L, DIFF_HEAD_DIM), 0.1),
        "diff_lambda_q2": nrm((L, DIFF_HEAD_DIM), 0.1),
        "diff_lambda_k2": nrm((L, DIFF_HEAD_DIM), 0.1),
        "diff_subln": gain((L, DIFF_V_DIM)),
        "mla_q_norm": gain((L, MLA_Q_RANK)),
        "mla_w_uq": nrm((L, MLA_Q_RANK, MLA_HEADS * MLA_QK_DIM), MLA_Q_RANK ** -0.5),
        "mla_kv_norm": gain((L, MLA_KV_RANK)),
        "mla_w_ukv": nrm((L, MLA_KV_RANK, MLA_HEADS * (MLA_NOPE_DIM + MLA_V_DIM)), MLA_KV_RANK ** -0.5),
        "mla_out_norm": gain((L, MLA_WIDTH)),
        "w_out": nrm((L, MIX_WIDTH, D), MIX_WIDTH ** -0.5),
        "ffn2_norm": gain((L, D)),
        "ffn2_w_gate": nrm((L, D, D_FF), D ** -0.5),
        "ffn2_w_up": nrm((L, D, D_FF), D ** -0.5),
        "ffn2_w_down": nrm((L, D_FF, D), D_FF ** -0.5),
        "final_norm": gain((D,)),
    }


def reference(x, c, positions, w_ada, b_ada,
              ffn1_norm, ffn1_w_gate, ffn1_w_up, ffn1_w_down,
              mix_norm, w_in, diff_lambda_q1, diff_lambda_k1, diff_lambda_q2, diff_lambda_k2,
              diff_subln, mla_q_norm, mla_w_uq, mla_kv_norm, mla_w_ukv, mla_out_norm, w_out,
              ffn2_norm, ffn2_w_gate, ffn2_w_up, ffn2_w_down, final_norm):
    B, S, D = x.shape
    cos_d, sin_d = rope_tables(positions, DIFF_HEAD_DIM)
    cos_r, sin_r = rope_tables(positions, MLA_ROPE_DIM)
    c_act = jax.nn.silu(c)
    h = x
    for l in range(DEPTH):
        mod = (c_act @ w_ada[l] + b_ada[l]).reshape(B, N_MOD, D)[:, :, None, :]
        sh1, sc1, g1, sh2, sc2, g2, sh3, sc3, g3 = [mod[:, j] for j in range(N_MOD)]
        u = modulate(rmsnorm(h, ffn1_norm[l]), sh1, sc1)
        h = h + 0.5 * g1 * swiglu(u, ffn1_w_gate[l], ffn1_w_up[l], ffn1_w_down[l])
        u = modulate(rmsnorm(h, mix_norm[l]), sh2, sc2)
        h = h + g2 * hybrid_mixer(u, cos_d, sin_d, cos_r, sin_r, l,
                                  w_in[l], diff_lambda_q1[l], diff_lambda_k1[l],
                                  diff_lambda_q2[l], diff_lambda_k2[l], diff_subln[l],
                                  mla_q_norm[l], mla_w_uq[l], mla_kv_norm[l], mla_w_ukv[l],
                                  mla_out_norm[l], w_out[l])
        u = modulate(rmsnorm(h, ffn2_norm[l]), sh3, sc3)
        h = h + 0.5 * g3 * swiglu(u, ffn2_w_gate[l], ffn2_w_up[l], ffn2_w_down[l])
    return rmsnorm(h, final_norm)
```

```python
import functools
import math

import jax
import jax.numpy as jnp
from jax import lax
from jax.experimental import pallas as pl
from jax.experimental.pallas import tpu as pltpu

F32 = jnp.float32
BF16 = jnp.bfloat16

D_MODEL = 1024
D_FF = 2816
N_MOD = 9
HEADS = 4
HEAD_LANES = 128
DIFF_HEAD_DIM = 64
MLA_NOPE_DIM = 64
MLA_ROPE_DIM = 32
MLA_QK_DIM = MLA_NOPE_DIM + MLA_ROPE_DIM
MLA_V_DIM = 128
MLA_Q_RANK = 384
MLA_KV_RANK = 256
GROUP_WIDTH = HEADS * HEAD_LANES
ROPE_THETA = 10000.0
NORM_EPS = 1e-6
LAMBDA_INIT = 0.8 - 0.6 * math.exp(-0.3 * 0)
MASK_VALUE = -1e30

TOKEN_TILE = 512
FF_CHUNK = 1408
ATTN_TILE = 256
VMEM_LIMIT_BYTES = 56 * 1024 * 1024


def _compiler_params(n_axes):
    return pltpu.CompilerParams(
        dimension_semantics=("parallel",) * n_axes,
        vmem_limit_bytes=VMEM_LIMIT_BYTES)


def _resident(shape):
    return pl.BlockSpec(shape, lambda *_: (0,) * len(shape),
                        pipeline_mode=pl.Buffered(1))


def _rmsnorm(x, g):
    return x * lax.rsqrt(jnp.mean(x * x, axis=-1, keepdims=True) + NORM_EPS) * g


def _bdot(a, b):
    return jnp.dot(a, b, preferred_element_type=F32)


def _mod_kernel(c_ref, w_ref, b_ref, o_ref):
    c_act = jax.nn.silu(c_ref[...]).astype(BF16)
    o_ref[...] = _bdot(c_act, w_ref[...].astype(BF16)) + b_ref[...]


def _adaln_mod(c, w_ada, b_ada):
    batch = c.shape[0]
    n_cols = w_ada.shape[1]
    col_tile = D_MODEL
    return pl.pallas_call(
        _mod_kernel,
        grid=(n_cols // col_tile,),
        in_specs=[pl.BlockSpec((batch, D_MODEL), lambda j: (0, 0)),
                  pl.BlockSpec((D_MODEL, col_tile), lambda j: (0, j)),
                  pl.BlockSpec((1, col_tile), lambda j: (0, j))],
        out_specs=pl.BlockSpec((batch, col_tile), lambda j: (0, j)),
        out_shape=jax.ShapeDtypeStruct((batch, n_cols), F32),
        compiler_params=_compiler_params(1),
        name="adaln_mod",
    )(c, w_ada, b_ada.reshape(1, n_cols))


def _rope_table_kernel(pos_ref, fd_ref, fr_ref, cosd_ref, sind_ref, cosr_ref, sinr_ref):
    pos = pos_ref[0].astype(F32)
    ang_d = fd_ref[...] * pos
    ang_r = fr_ref[...] * pos
    cd, sd = jnp.cos(ang_d), jnp.sin(ang_d)
    cr, sr = jnp.cos(ang_r), jnp.sin(ang_r)
    seq = pos.shape[1]
    ones = jnp.ones((MLA_NOPE_DIM, seq), F32)
    zeros_nope = jnp.zeros((MLA_NOPE_DIM, seq), F32)
    zeros_pad = jnp.zeros((HEAD_LANES - MLA_QK_DIM, seq), F32)
    cosd_ref[0] = jnp.concatenate([cd, cd, cd, cd], axis=0).T
    sind_ref[0] = jnp.concatenate([-sd, sd, -sd, sd], axis=0).T
    cosr_ref[0] = jnp.concatenate([ones, cr, cr, zeros_pad], axis=0).T
    sinr_ref[0] = jnp.concatenate([zeros_nope, -sr, sr, zeros_pad], axis=0).T


def _rope_tables(positions):
    batch, seq = positions.shape
    inv_d = ROPE_THETA ** (-jnp.arange(0, DIFF_HEAD_DIM, 2, dtype=F32) / DIFF_HEAD_DIM)
    inv_r = ROPE_THETA ** (-jnp.arange(0, MLA_ROPE_DIM, 2, dtype=F32) / MLA_ROPE_DIM)
    table = jax.ShapeDtypeStruct((batch, seq, HEAD_LANES), F32)
    table_spec = pl.BlockSpec((1, seq, HEAD_LANES), lambda b: (b, 0, 0))
    return pl.pallas_call(
        _rope_table_kernel,
        grid=(batch,),
        in_specs=[pl.BlockSpec((1, 1, seq), lambda b: (b, 0, 0)),
                  pl.BlockSpec((DIFF_HEAD_DIM // 2, 1), lambda b: (0, 0)),
                  pl.BlockSpec((MLA_ROPE_DIM // 2, 1), lambda b: (0, 0))],
        out_specs=[table_spec] * 4,
        out_shape=[table] * 4,
        compiler_params=_compiler_params(1),
        name="rope_tables",
    )(positions.reshape(batch, 1, seq), inv_d.reshape(-1, 1), inv_r.reshape(-1, 1))


def _ffn_halfstep(h, shift, scale, gate, norm_w, wg_ref, wu_ref, wd_ref):
    u = (_rmsnorm(h, norm_w) * (1 + scale) + shift).astype(BF16)
    y = None
    for c0 in range(0, D_FF, FF_CHUNK):
        g = _bdot(u, wg_ref[:, c0:c0 + FF_CHUNK])
        up = _bdot(u, wu_ref[:, c0:c0 + FF_CHUNK])
        a = (jax.nn.silu(g) * up).astype(BF16)
        part = _bdot(a, wd_ref[c0:c0 + FF_CHUNK, :])
        y = part if y is None else y + part
    return h + (0.5 * gate) * y


def _ffn1_kernel(x_ref, mod_ref, nw_ref, wg_ref, wu_ref, wd_ref, o_ref):
    mod = mod_ref[0]
    o_ref[0] = _ffn_halfstep(x_ref[0], mod[0:1], mod[1:2], mod[2:3], nw_ref[...],
                             wg_ref, wu_ref, wd_ref)


def _ffn1(x, mod3, norm_w, wg, wu, wd):
    batch, seq, _ = x.shape
    tile = pl.BlockSpec((1, TOKEN_TILE, D_MODEL), lambda b, i: (b, i, 0))
    return pl.pallas_call(
        _ffn1_kernel,
        grid=(batch, seq // TOKEN_TILE),
        in_specs=[tile,
                  pl.BlockSpec((1, N_MOD, D_MODEL), lambda b, i: (b, 0, 0)),
                  _resident((1, D_MODEL)),
                  _resident((D_MODEL, D_FF)), _resident((D_MODEL, D_FF)),
                  _resident((D_FF, D_MODEL))],
        out_specs=tile,
        out_shape=jax.ShapeDtypeStruct(x.shape, F32),
        compiler_params=_compiler_params(2),
        name="ffn1",
    )(x, mod3, norm_w, wg, wu, wd)


_DQ0, _DK0, _DV0 = 0, GROUP_WIDTH, 2 * GROUP_WIDTH
_CQ0 = 3 * GROUP_WIDTH
_CKV0 = _CQ0 + MLA_Q_RANK
_KR0 = _CKV0 + MLA_KV_RANK
IN_COLS_PADDED = _KR0 + HEAD_LANES


def _swap_halves(x, lo_mask, half):
    lanes = x.shape[-1]
    return jnp.where(lo_mask, pltpu.roll(x, lanes - half, 1), pltpu.roll(x, half, 1))


def _inproj_kernel(h_ref, mod_ref, nw_ref, cosd_ref, sind_ref, cosr_ref, sinr_ref,
                   win_ref, qn_ref, kvn_ref, wuq_ref, wuk_ref, wuv_ref,
                   dq_ref, dk_ref, dv_ref, mq_ref, mk_ref, mv_ref):
    mod = mod_ref[0]
    u = (_rmsnorm(h_ref[0], nw_ref[...]) * (1 + mod[4:5]) + mod[3:4]).astype(BF16)
    proj = _bdot(u, win_ref[...])

    rows = proj.shape[0]
    lane = lax.broadcasted_iota(jnp.int32, (rows, HEAD_LANES), 1)
    lo_d = (lane % DIFF_HEAD_DIM) < (DIFF_HEAD_DIM // 2)
    lo_r = (lane >= MLA_NOPE_DIM) & (lane < MLA_NOPE_DIM + MLA_ROPE_DIM // 2)
    cosd, sind = cosd_ref[0], sind_ref[0]
    cosr, sinr = cosr_ref[0], sinr_ref[0]

    def rope_d(x):
        return x * cosd + _swap_halves(x, lo_d, DIFF_HEAD_DIM // 2) * sind

    def rope_r(x):
        return x * cosr + _swap_halves(x, lo_r, MLA_ROPE_DIM // 2) * sinr

    cq = _rmsnorm(proj[:, _CQ0:_CQ0 + MLA_Q_RANK], qn_ref[...]).astype(BF16)
    ckv = _rmsnorm(proj[:, _CKV0:_CKV0 + MLA_KV_RANK], kvn_ref[...]).astype(BF16)
    mq = _bdot(cq, wuq_ref[...])
    mk = _bdot(ckv, wuk_ref[...])
    mv = _bdot(ckv, wuv_ref[...])
    kr = rope_r(proj[:, _KR0:_KR0 + HEAD_LANES])

    diff_scale = DIFF_HEAD_DIM ** -0.5
    for hd in range(HEADS):
        c0 = hd * HEAD_LANES
        cols = slice(c0, c0 + HEAD_LANES)
        dq_ref[0, :, cols] = (rope_d(proj[:, _DQ0 + c0:_DQ0 + c0 + HEAD_LANES]) * diff_scale).astype(BF16)
        dk_ref[0, :, cols] = rope_d(proj[:, _DK0 + c0:_DK0 + c0 + HEAD_LANES]).astype(BF16)
        mq_ref[0, :, cols] = rope_r(mq[:, cols]).astype(BF16)
        mk_ref[0, :, cols] = (mk[:, cols] + kr).astype(BF16)
    dv_ref[0] = proj[:, _DV0:_DV0 + GROUP_WIDTH].astype(BF16)
    mv_ref[0] = mv.astype(BF16)


def _in_proj(h1, mod3, norm_w, tables, w_in_p, q_norm, kv_norm, wuq_p, wuk_p, wuv_p):
    batch, seq, _ = h1.shape
    idx = lambda b, i: (b, i, 0)
    group = jax.ShapeDtypeStruct((batch, seq, GROUP_WIDTH), BF16)
    group_spec = pl.BlockSpec((1, TOKEN_TILE, GROUP_WIDTH), idx)
    table_spec = pl.BlockSpec((1, TOKEN_TILE, HEAD_LANES), idx)
    return pl.pallas_call(
        _inproj_kernel,
        grid=(batch, seq // TOKEN_TILE),
        in_specs=[pl.BlockSpec((1, TOKEN_TILE, D_MODEL), idx),
                  pl.BlockSpec((1, N_MOD, D_MODEL), lambda b, i: (b, 0, 0)),
                  _resident((1, D_MODEL)),
                  table_spec, table_spec, table_spec, table_spec,
                  _resident((D_MODEL, IN_COLS_PADDED)),
                  _resident((1, MLA_Q_RANK)), _resident((1, MLA_KV_RANK)),
                  _resident((MLA_Q_RANK, GROUP_WIDTH)),
                  _resident((MLA_KV_RANK, GROUP_WIDTH)),
                  _resident((MLA_KV_RANK, GROUP_WIDTH))],
        out_specs=[group_spec] * 6,
        out_shape=[group] * 6,
        compiler_params=_compiler_params(2),
        name="in_proj",
    )(h1, mod3, norm_w, *tables, w_in_p, q_norm, kv_norm, wuq_p, wuk_p, wuv_p)


def _attend(q_rows, k_ref, v_ref, q_block, score_scale):
    n_rows = q_rows.shape[0]

    def block(j, carry, diagonal):
        m, l, acc = carry
        start = pl.multiple_of(j * ATTN_TILE, ATTN_TILE)
        k = k_ref[0, pl.ds(start, ATTN_TILE), :]
        v = v_ref[0, pl.ds(start, ATTN_TILE), :]
        s = lax.dot_general(q_rows, k, (((1,), (1,)), ((), ())), preferred_element_type=F32)
        if score_scale is not None:
            s = s * score_scale
        if diagonal:
            row = lax.broadcasted_iota(jnp.int32, s.shape, 0) % ATTN_TILE
            col = lax.broadcasted_iota(jnp.int32, s.shape, 1)
            s = jnp.where(col <= row, s, MASK_VALUE)
        m_new = jnp.maximum(m, jnp.max(s, axis=1, keepdims=True))
        p = jnp.exp(s - m_new)
        alpha = jnp.exp(m - m_new)
        l = alpha * l + jnp.sum(p, axis=1, keepdims=True)
        acc = alpha * acc + _bdot(p.astype(BF16), v)
        return m_new, l, acc

    init = (jnp.full((n_rows, 1), MASK_VALUE, F32),
            jnp.zeros((n_rows, 1), F32),
            jnp.zeros((n_rows, HEAD_LANES), F32))
    carry = lax.fori_loop(0, q_block, lambda j, c: block(j, c, False), init)
    _, l, acc = block(q_block, carry, True)
    return acc, l


def _diff_attn_kernel(q_ref, k_ref, v_ref, lq1_ref, lk1_ref, lq2_ref, lk2_ref, subln_ref, o_ref):
    q = q_ref[0]
    lane = lax.broadcasted_iota(jnp.int32, q.shape, 1)
    first_map = lane < DIFF_HEAD_DIM
    zero = jnp.zeros_like(q)
    q_both = jnp.concatenate([jnp.where(first_map, q, zero), jnp.where(first_map, zero, q)], axis=0)
    acc, l = _attend(q_both, k_ref, v_ref, pl.program_id(2), None)
    lam = (jnp.exp(jnp.sum(lq1_ref[...] * lk1_ref[...], keepdims=True))
           - jnp.exp(jnp.sum(lq2_ref[...] * lk2_ref[...], keepdims=True)) + LAMBDA_INIT)
    out = acc / l
    od = out[:ATTN_TILE] - lam * out[ATTN_TILE:]
    o_ref[0] = (_rmsnorm(od, subln_ref[...]) * (1 - LAMBDA_INIT)).astype(BF16)


def _mla_attn_kernel(q_ref, k_ref, v_ref, o_ref):
    acc, l = _attend(q_ref[0], k_ref, v_ref, pl.program_id(2), MLA_QK_DIM ** -0.5)
    o_ref[0] = (acc / l).astype(BF16)


def _attention(kernel, q, k, v, extra, name):
    batch, seq, _ = q.shape
    q_spec = pl.BlockSpec((1, ATTN_TILE, HEAD_LANES), lambda b, h, i: (b, i, h))
    kv_spec = pl.BlockSpec((1, seq, HEAD_LANES), lambda b, h, i: (b, 0, h))
    extra_specs = [pl.BlockSpec(e.shape, lambda b, h, i: (0, 0)) for e in extra]
    return pl.pallas_call(
        kernel,
        grid=(batch, HEADS, seq // ATTN_TILE),
        in_specs=[q_spec, kv_spec, kv_spec] + extra_specs,
        out_specs=q_spec,
        out_shape=jax.ShapeDtypeStruct((batch, seq, GROUP_WIDTH), BF16),
        compiler_params=_compiler_params(3),
        name=name,
    )(q, k, v, *extra)


def _out_ffn2_kernel(od_ref, om_ref, h_ref, mod_ref, onorm_ref, woa_ref, wob_ref,
                     nw_ref, wg_ref, wu_ref, wd_ref, fnorm_ref, o_ref):
    mod = mod_ref[0]
    om = _rmsnorm(om_ref[0].astype(F32), onorm_ref[...]).astype(BF16)
    mixed = _bdot(od_ref[0], woa_ref[...]) + _bdot(om, wob_ref[...])
    h2 = h_ref[0] + mod[5:6] * mixed
    h3 = _ffn_halfstep(h2, mod[6:7], mod[7:8], mod[8:9], nw_ref[...], wg_ref, wu_ref, wd_ref)
    o_ref[0] = _rmsnorm(h3, fnorm_ref[...])


def _out_ffn2(od, om, h1, mod3, out_norm, wo_a, wo_b, norm_w, wg, wu, wd, final_norm):
    batch, seq, _ = h1.shape
    idx = lambda b, i: (b, i, 0)
    tile = pl.BlockSpec((1, TOKEN_TILE, D_MODEL), idx)
    group_spec = pl.BlockSpec((1, TOKEN_TILE, GROUP_WIDTH), idx)
    return pl.pallas_call(
        _out_ffn2_kernel,
        grid=(batch, seq // TOKEN_TILE),
        in_specs=[group_spec, group_spec, tile,
                  pl.BlockSpec((1, N_MOD, D_MODEL), lambda b, i: (b, 0, 0)),
                  _resident((1, GROUP_WIDTH)),
                  _resident((GROUP_WIDTH, D_MODEL)), _resident((GROUP_WIDTH, D_MODEL)),
                  _resident((1, D_MODEL)),
                  _resident((D_MODEL, D_FF)), _resident((D_MODEL, D_FF)),
                  _resident((D_FF, D_MODEL)),
                  _resident((1, D_MODEL))],
        out_specs=tile,
        out_shape=jax.ShapeDtypeStruct(h1.shape, F32),
        compiler_params=_compiler_params(2),
        name="out_ffn2",
    )(od, om, h1, mod3, out_norm, wo_a, wo_b, norm_w, wg, wu, wd, final_norm)


def _pad_cols(w, left, total):
    return jnp.pad(w, ((0, 0), (left, total - left - w.shape[1])))


def _layout_weights(w_in, w_uq, w_ukv):
    split = 3 * GROUP_WIDTH + MLA_Q_RANK + MLA_KV_RANK
    w_in_p = jnp.concatenate(
        [w_in[:, :split], _pad_cols(w_in[:, split:], MLA_NOPE_DIM, HEAD_LANES)], axis=1).astype(BF16)
    uq = w_uq.reshape(MLA_Q_RANK, HEADS, MLA_QK_DIM)
    wuq_p = jnp.pad(uq, ((0, 0), (0, 0), (0, HEAD_LANES - MLA_QK_DIM))).reshape(MLA_Q_RANK, GROUP_WIDTH)
    ukv = w_ukv.reshape(MLA_KV_RANK, HEADS, MLA_NOPE_DIM + MLA_V_DIM)
    wuk_p = jnp.pad(ukv[:, :, :MLA_NOPE_DIM],
                    ((0, 0), (0, 0), (0, HEAD_LANES - MLA_NOPE_DIM))).reshape(MLA_KV_RANK, GROUP_WIDTH)
    wuv_p = ukv[:, :, MLA_NOPE_DIM:].reshape(MLA_KV_RANK, GROUP_WIDTH)
    return w_in_p, wuq_p.astype(BF16), wuk_p.astype(BF16), wuv_p.astype(BF16)


def kernel(x, c, positions, w_ada, b_ada, ffn1_norm, ffn1_w_gate, ffn1_w_up, ffn1_w_down, mix_norm, w_in, diff_lambda_q1, diff_lambda_k1, diff_lambda_q2, diff_lambda_k2, diff_subln, mla_q_norm, mla_w_uq, mla_kv_norm, mla_w_ukv, mla_out_norm, w_out, ffn2_norm, ffn2_w_gate, ffn2_w_up, ffn2_w_down, final_norm):
    batch = x.shape[0]
    assert w_ada.shape[0] == 1, "single-layer model"
    mod3 = _adaln_mod(c, w_ada[0], b_ada[0]).reshape(batch, N_MOD, D_MODEL)
    tables = _rope_tables(positions)

    h1 = _ffn1(x, mod3, ffn1_norm, ffn1_w_gate[0].astype(BF16), ffn1_w_up[0].astype(BF16),
               ffn1_w_down[0].astype(BF16))

    w_in_p, wuq_p, wuk_p, wuv_p = _layout_weights(w_in[0], mla_w_uq[0], mla_w_ukv[0])
    dq, dk, dv, mq, mk, mv = _in_proj(h1, mod3, mix_norm, tables, w_in_p,
                                      mla_q_norm, mla_kv_norm, wuq_p, wuk_p, wuv_p)

    od = _attention(_diff_attn_kernel, dq, dk, dv,
                    [diff_lambda_q1, diff_lambda_k1, diff_lambda_q2, diff_lambda_k2, diff_subln],
                    "diff_attn")
    om = _attention(_mla_attn_kernel, mq, mk, mv, [], "mla_attn")

    w_out_b = w_out[0].astype(BF16)
    return _out_ffn2(od, om, h1, mod3, mla_out_norm, w_out_b[:GROUP_WIDTH], w_out_b[GROUP_WIDTH:],
                     ffn2_norm, ffn2_w_gate[0].astype(BF16), ffn2_w_up[0].astype(BF16),
                     ffn2_w_down[0].astype(BF16), final_norm.reshape(1, D_MODEL))
```

```python
import functools
import math

import jax
import jax.numpy as jnp
from jax import lax
from jax.experimental import pallas as pl
from jax.experimental.pallas import tpu as pltpu

F32 = jnp.float32
BF16 = jnp.bfloat16

D_MODEL = 1024
D_FF = 2816
N_MOD = 9
HEADS = 4
HEAD_LANES = 128
DIFF_HEAD_DIM = 64
MLA_NOPE_DIM = 64
MLA_ROPE_DIM = 32
MLA_QK_DIM = MLA_NOPE_DIM + MLA_ROPE_DIM
MLA_V_DIM = 128
MLA_Q_RANK = 384
MLA_KV_RANK = 256
GROUP_WIDTH = HEADS * HEAD_LANES
ROPE_THETA = 10000.0
NORM_EPS = 1e-6
LAMBDA_INIT = 0.8 - 0.6 * math.exp(-0.3 * 0)
MASK_VALUE = -1e30
LOG2_E = math.log2(math.e)

TOKEN_TILE = 512
FF_CHUNK = 1408
ATTN_TILE = 256
VMEM_LIMIT_BYTES = 56 * 1024 * 1024


def _compiler_params(n_axes):
    return pltpu.CompilerParams(
        dimension_semantics=("parallel",) * n_axes,
        vmem_limit_bytes=VMEM_LIMIT_BYTES)


def _resident(shape):
    return pl.BlockSpec(shape, lambda *_: (0,) * len(shape),
                        pipeline_mode=pl.Buffered(1))


def _rmsnorm(x, g):
    return x * lax.rsqrt(jnp.mean(x * x, axis=-1, keepdims=True) + NORM_EPS) * g


def _bdot(a, b):
    return jnp.dot(a, b, preferred_element_type=F32)


def _mod_kernel(c_ref, w_ref, b_ref, o_ref):
    c_act = jax.nn.silu(c_ref[...]).astype(BF16)
    o_ref[...] = _bdot(c_act, w_ref[...].astype(BF16)) + b_ref[...]


def _adaln_mod(c, w_ada, b_ada):
    batch = c.shape[0]
    n_cols = w_ada.shape[1]
    col_tile = D_MODEL
    return pl.pallas_call(
        _mod_kernel,
        grid=(n_cols // col_tile,),
        in_specs=[pl.BlockSpec((batch, D_MODEL), lambda j: (0, 0)),
                  pl.BlockSpec((D_MODEL, col_tile), lambda j: (0, j)),
                  pl.BlockSpec((1, col_tile), lambda j: (0, j))],
        out_specs=pl.BlockSpec((batch, col_tile), lambda j: (0, j)),
        out_shape=jax.ShapeDtypeStruct((batch, n_cols), F32),
        compiler_params=_compiler_params(1),
        name="adaln_mod",
    )(c, w_ada, b_ada.reshape(1, n_cols))


def _rope_table_kernel(pos_ref, fd_ref, fr_ref, cosd_ref, sind_ref, cosr_ref, sinr_ref):
    pos = pos_ref[0].astype(F32)
    ang_d = fd_ref[...] * pos
    ang_r = fr_ref[...] * pos
    cd, sd = jnp.cos(ang_d), jnp.sin(ang_d)
    cr, sr = jnp.cos(ang_r), jnp.sin(ang_r)
    seq = pos.shape[1]
    ones = jnp.ones((MLA_NOPE_DIM, seq), F32)
    zeros_nope = jnp.zeros((MLA_NOPE_DIM, seq), F32)
    zeros_pad = jnp.zeros((HEAD_LANES - MLA_QK_DIM, seq), F32)
    cosd_ref[0] = jnp.concatenate([cd, cd, cd, cd], axis=0).T
    sind_ref[0] = jnp.concatenate([-sd, sd, -sd, sd], axis=0).T
    cosr_ref[0] = jnp.concatenate([ones, cr, cr, zeros_pad], axis=0).T
    sinr_ref[0] = jnp.concatenate([zeros_nope, -sr, sr, zeros_pad], axis=0).T


def _rope_tables(positions):
    batch, seq = positions.shape
    inv_d = ROPE_THETA ** (-jnp.arange(0, DIFF_HEAD_DIM, 2, dtype=F32) / DIFF_HEAD_DIM)
    inv_r = ROPE_THETA ** (-jnp.arange(0, MLA_ROPE_DIM, 2, dtype=F32) / MLA_ROPE_DIM)
    table = jax.ShapeDtypeStruct((batch, seq, HEAD_LANES), F32)
    table_spec = pl.BlockSpec((1, seq, HEAD_LANES), lambda b: (b, 0, 0))
    return pl.pallas_call(
        _rope_table_kernel,
        grid=(batch,),
        in_specs=[pl.BlockSpec((1, 1, seq), lambda b: (b, 0, 0)),
                  pl.BlockSpec((DIFF_HEAD_DIM // 2, 1), lambda b: (0, 0)),
                  pl.BlockSpec((MLA_ROPE_DIM // 2, 1), lambda b: (0, 0))],
        out_specs=[table_spec] * 4,
        out_shape=[table] * 4,
        compiler_params=_compiler_params(1),
        name="rope_tables",
    )(positions.reshape(batch, 1, seq), inv_d.reshape(-1, 1), inv_r.reshape(-1, 1))


def _ffn_halfstep(h, shift, scale, gate, norm_w, wg_ref, wu_ref, wd_ref):
    u = (_rmsnorm(h, norm_w) * (1 + scale) + shift).astype(BF16)
    y = None
    for c0 in range(0, D_FF, FF_CHUNK):
        g = _bdot(u, wg_ref[:, c0:c0 + FF_CHUNK])
        up = _bdot(u, wu_ref[:, c0:c0 + FF_CHUNK])
        a = (jax.nn.silu(g) * up).astype(BF16)
        part = _bdot(a, wd_ref[c0:c0 + FF_CHUNK, :])
        y = part if y is None else y + part
    return h + (0.5 * gate) * y


def _ffn1_kernel(x_ref, mod_ref, nw_ref, wg_ref, wu_ref, wd_ref, o_ref):
    mod = mod_ref[0]
    o_ref[0] = _ffn_halfstep(x_ref[0], mod[0:1], mod[1:2], mod[2:3], nw_ref[...],
                             wg_ref, wu_ref, wd_ref)


def _ffn1(x, mod3, norm_w, wg, wu, wd):
    batch, seq, _ = x.shape
    tile = pl.BlockSpec((1, TOKEN_TILE, D_MODEL), lambda b, i: (b, i, 0))
    return pl.pallas_call(
        _ffn1_kernel,
        grid=(batch, seq // TOKEN_TILE),
        in_specs=[tile,
                  pl.BlockSpec((1, N_MOD, D_MODEL), lambda b, i: (b, 0, 0)),
                  _resident((1, D_MODEL)),
                  _resident((D_MODEL, D_FF)), _resident((D_MODEL, D_FF)),
                  _resident((D_FF, D_MODEL))],
        out_specs=tile,
        out_shape=jax.ShapeDtypeStruct(x.shape, F32),
        compiler_params=_compiler_params(2),
        name="ffn1",
    )(x, mod3, norm_w, wg, wu, wd)


_DQ0, _DK0, _DV0 = 0, GROUP_WIDTH, 2 * GROUP_WIDTH
_CQ0 = 3 * GROUP_WIDTH
_CKV0 = _CQ0 + MLA_Q_RANK
_KR0 = _CKV0 + MLA_KV_RANK
IN_COLS_PADDED = _KR0 + HEAD_LANES


def _swap_halves(x, lo_mask, half):
    lanes = x.shape[-1]
    return jnp.where(lo_mask, pltpu.roll(x, lanes - half, 1), pltpu.roll(x, half, 1))


def _inproj_kernel(h_ref, mod_ref, nw_ref, cosd_ref, sind_ref, cosr_ref, sinr_ref,
                   win_ref, qn_ref, kvn_ref, wuq_ref, wuk_ref, wuv_ref,
                   dq_ref, dk_ref, dv_ref, mq_ref, mk_ref, mv_ref):
    mod = mod_ref[0]
    u = (_rmsnorm(h_ref[0], nw_ref[...]) * (1 + mod[4:5]) + mod[3:4]).astype(BF16)
    proj = _bdot(u, win_ref[...])

    rows = proj.shape[0]
    lane = lax.broadcasted_iota(jnp.int32, (rows, HEAD_LANES), 1)
    lo_d = (lane % DIFF_HEAD_DIM) < (DIFF_HEAD_DIM // 2)
    lo_r = (lane >= MLA_NOPE_DIM) & (lane < MLA_NOPE_DIM + MLA_ROPE_DIM // 2)
    cosd, sind = cosd_ref[0], sind_ref[0]
    cosr, sinr = cosr_ref[0], sinr_ref[0]

    def rope_d(x):
        return x * cosd + _swap_halves(x, lo_d, DIFF_HEAD_DIM // 2) * sind

    def rope_r(x):
        return x * cosr + _swap_halves(x, lo_r, MLA_ROPE_DIM // 2) * sinr

    cq = _rmsnorm(proj[:, _CQ0:_CQ0 + MLA_Q_RANK], qn_ref[...]).astype(BF16)
    ckv = _rmsnorm(proj[:, _CKV0:_CKV0 + MLA_KV_RANK], kvn_ref[...]).astype(BF16)
    mq = _bdot(cq, wuq_ref[...])
    mk = _bdot(ckv, wuk_ref[...])
    mv = _bdot(ckv, wuv_ref[...])
    kr = rope_r(proj[:, _KR0:_KR0 + HEAD_LANES])

    diff_scale = DIFF_HEAD_DIM ** -0.5 * LOG2_E
    mla_scale = MLA_QK_DIM ** -0.5 * LOG2_E
    for hd in range(HEADS):
        c0 = hd * HEAD_LANES
        cols = slice(c0, c0 + HEAD_LANES)
        dq_ref[0, :, cols] = (rope_d(proj[:, _DQ0 + c0:_DQ0 + c0 + HEAD_LANES]) * diff_scale).astype(BF16)
        dk_ref[0, :, cols] = rope_d(proj[:, _DK0 + c0:_DK0 + c0 + HEAD_LANES]).astype(BF16)
        mq_ref[0, :, cols] = (rope_r(mq[:, cols]) * mla_scale).astype(BF16)
        mk_ref[0, :, cols] = (mk[:, cols] + kr).astype(BF16)
    dv_ref[0] = proj[:, _DV0:_DV0 + GROUP_WIDTH].astype(BF16)
    mv_ref[0] = mv.astype(BF16)


def _in_proj(h1, mod3, norm_w, tables, w_in_p, q_norm, kv_norm, wuq_p, wuk_p, wuv_p):
    batch, seq, _ = h1.shape
    idx = lambda b, i: (b, i, 0)
    group = jax.ShapeDtypeStruct((batch, seq, GROUP_WIDTH), BF16)
    group_spec = pl.BlockSpec((1, TOKEN_TILE, GROUP_WIDTH), idx)
    table_spec = pl.BlockSpec((1, TOKEN_TILE, HEAD_LANES), idx)
    return pl.pallas_call(
        _inproj_kernel,
        grid=(batch, seq // TOKEN_TILE),
        in_specs=[pl.BlockSpec((1, TOKEN_TILE, D_MODEL), idx),
                  pl.BlockSpec((1, N_MOD, D_MODEL), lambda b, i: (b, 0, 0)),
                  _resident((1, D_MODEL)),
                  table_spec, table_spec, table_spec, table_spec,
                  _resident((D_MODEL, IN_COLS_PADDED)),
                  _resident((1, MLA_Q_RANK)), _resident((1, MLA_KV_RANK)),
                  _resident((MLA_Q_RANK, GROUP_WIDTH)),
                  _resident((MLA_KV_RANK, GROUP_WIDTH)),
                  _resident((MLA_KV_RANK, GROUP_WIDTH))],
        out_specs=[group_spec] * 6,
        out_shape=[group] * 6,
        compiler_params=_compiler_params(2),
        name="in_proj",
    )(h1, mod3, norm_w, *tables, w_in_p, q_norm, kv_norm, wuq_p, wuk_p, wuv_p)


SCORE_GROUPS = 2
SCORE_WIDTH = SCORE_GROUPS * ATTN_TILE


def _causal_attention(q_ref, k_ref, v_ref, vt_ref, s2_ref, p2_ref, make_q_t, value_rows, emit):
    seq = q_ref.shape[1]
    n_blocks = seq // ATTN_TILE
    vt_ref[...] = v_ref[0].astype(F32).T.astype(BF16)
    key = lax.broadcasted_iota(jnp.int32, (ATTN_TILE, SCORE_WIDTH), 0)
    query = lax.broadcasted_iota(jnp.int32, (ATTN_TILE, SCORE_WIDTH), 1) % ATTN_TILE
    causal = key <= query

    def scores(i):
        q0, kend = i * ATTN_TILE, (i + 1) * ATTN_TILE
        s_ref = s2_ref.at[i % 2]
        q_t = make_q_t(q_ref[0, q0:kend, :].astype(F32))
        s = _bdot(k_ref[0, 0:kend, :], q_t)
        if i > 0:
            s_ref[0:q0, :] = s[0:q0]
        s_ref[q0:kend, :] = jnp.where(causal, s[q0:kend], MASK_VALUE)
        return jnp.max(s_ref[0:kend, :], axis=0, keepdims=True)

    def finish(i, m):
        kend = (i + 1) * ATTN_TILE
        s_ref, p_ref = s2_ref.at[i % 2], p2_ref.at[i % 2]
        p = jnp.exp2(s_ref[0:kend, :] - m)
        l = jnp.sum(p, axis=0, keepdims=True)
        p_ref[0:kend, :] = p.astype(BF16)
        outs = []
        for g in range(SCORE_GROUPS):
            lanes = slice(g * ATTN_TILE, (g + 1) * ATTN_TILE)
            outs.append(_bdot(vt_ref[value_rows[g], 0:kend], p_ref[0:kend, lanes]) / l[:, lanes])
        emit(i, outs)

    m = scores(0)
    for i in range(n_blocks):
        m_next = scores(i + 1) if i + 1 < n_blocks else None
        finish(i, m)
        m = m_next


def _diff_attn_kernel(q_ref, k_ref, v_ref, lq1_ref, lk1_ref, lq2_ref, lk2_ref, subln_ref, o_ref,
                      vt_ref, s2_ref, p2_ref):
    lane = lax.broadcasted_iota(jnp.int32, (ATTN_TILE, HEAD_LANES), 1)
    first_map = lane < DIFF_HEAD_DIM
    lam = (jnp.exp(jnp.sum(lq1_ref[...] * lk1_ref[...], keepdims=True))
           - jnp.exp(jnp.sum(lq2_ref[...] * lk2_ref[...], keepdims=True)) + LAMBDA_INIT)

    def make_q_t(q):
        return jnp.concatenate([jnp.where(first_map, q, 0.0).T, jnp.where(first_map, 0.0, q).T],
                               axis=1).astype(BF16)

    def emit(i, outs):
        od = (outs[0] - lam * outs[1]).T
        o_ref[0, i * ATTN_TILE:(i + 1) * ATTN_TILE, :] = (
            _rmsnorm(od, subln_ref[...]) * (1 - LAMBDA_INIT)).astype(BF16)

    whole = slice(0, HEAD_LANES)
    _causal_attention(q_ref, k_ref, v_ref, vt_ref, s2_ref, p2_ref, make_q_t, [whole, whole], emit)


def _mla_attn_kernel(q_ref, k_ref, v_ref, o_ref, vt_ref, s2_ref, p2_ref):
    lane = lax.broadcasted_iota(jnp.int32, (ATTN_TILE, SCORE_GROUPS * HEAD_LANES), 1)
    head_rows = [slice(g * HEAD_LANES, (g + 1) * HEAD_LANES) for g in range(SCORE_GROUPS)]

    def make_q_t(q):
        return jnp.concatenate([jnp.where(lane // HEAD_LANES == g, q, 0.0).T for g in range(SCORE_GROUPS)],
                               axis=1).astype(BF16)

    def emit(i, outs):
        for g in range(SCORE_GROUPS):
            o_ref[0, i * ATTN_TILE:(i + 1) * ATTN_TILE, head_rows[g]] = outs[g].T.astype(BF16)

    _causal_attention(q_ref, k_ref, v_ref, vt_ref, s2_ref, p2_ref, make_q_t, head_rows, emit)


def _attention(kernel, heads_per_step, q, k, v, extra, name):
    batch, seq, _ = q.shape
    lanes = heads_per_step * HEAD_LANES
    spec = pl.BlockSpec((1, seq, lanes), lambda b, h: (b, 0, h))
    extra_specs = [pl.BlockSpec(e.shape, lambda b, h: (0, 0)) for e in extra]
    return pl.pallas_call(
        kernel,
        grid=(batch, HEADS // heads_per_step),
        in_specs=[spec, spec, spec] + extra_specs,
        out_specs=spec,
        out_shape=jax.ShapeDtypeStruct((batch, seq, GROUP_WIDTH), BF16),
        scratch_shapes=[pltpu.VMEM((lanes, seq), BF16),
                        pltpu.VMEM((2, seq, SCORE_WIDTH), F32),
                        pltpu.VMEM((2, seq, SCORE_WIDTH), BF16)],
        compiler_params=_compiler_params(2),
        name=name,
    )(q, k, v, *extra)


def _out_ffn2_kernel(od_ref, om_ref, h_ref, mod_ref, onorm_ref, woa_ref, wob_ref,
                     nw_ref, wg_ref, wu_ref, wd_ref, fnorm_ref, o_ref):
    mod = mod_ref[0]
    om = _rmsnorm(om_ref[0].astype(F32), onorm_ref[...]).astype(BF16)
    mixed = _bdot(od_ref[0], woa_ref[...]) + _bdot(om, wob_ref[...])
    h2 = h_ref[0] + mod[5:6] * mixed
    h3 = _ffn_halfstep(h2, mod[6:7], mod[7:8], mod[8:9], nw_ref[...], wg_ref, wu_ref, wd_ref)
    o_ref[0] = _rmsnorm(h3, fnorm_ref[...])


def _out_ffn2(od, om, h1, mod3, out_norm, wo_a, wo_b, norm_w, wg, wu, wd, final_norm):
    batch, seq, _ = h1.shape
    idx = lambda b, i: (b, i, 0)
    tile = pl.BlockSpec((1, TOKEN_TILE, D_MODEL), idx)
    group_spec = pl.BlockSpec((1, TOKEN_TILE, GROUP_WIDTH), idx)
    return pl.pallas_call(
        _out_ffn2_kernel,
        grid=(batch, seq // TOKEN_TILE),
        in_specs=[group_spec, group_spec, tile,
                  pl.BlockSpec((1, N_MOD, D_MODEL), lambda b, i: (b, 0, 0)),
                  _resident((1, GROUP_WIDTH)),
                  _resident((GROUP_WIDTH, D_MODEL)), _resident((GROUP_WIDTH, D_MODEL)),
                  _resident((1, D_MODEL)),
                  _resident((D_MODEL, D_FF)), _resident((D_MODEL, D_FF)),
                  _resident((D_FF, D_MODEL)),
                  _resident((1, D_MODEL))],
        out_specs=tile,
        out_shape=jax.ShapeDtypeStruct(h1.shape, F32),
        compiler_params=_compiler_params(2),
        name="out_ffn2",
    )(od, om, h1, mod3, out_norm, wo_a, wo_b, norm_w, wg, wu, wd, final_norm)


def _pad_cols(w, left, total):
    return jnp.pad(w, ((0, 0), (left, total - left - w.shape[1])))


def _layout_weights(w_in, w_uq, w_ukv):
    split = 3 * GROUP_WIDTH + MLA_Q_RANK + MLA_KV_RANK
    w_in_p = jnp.concatenate(
        [w_in[:, :split], _pad_cols(w_in[:, split:], MLA_NOPE_DIM, HEAD_LANES)], axis=1).astype(BF16)
    uq = w_uq.reshape(MLA_Q_RANK, HEADS, MLA_QK_DIM)
    wuq_p = jnp.pad(uq, ((0, 0), (0, 0), (0, HEAD_LANES - MLA_QK_DIM))).reshape(MLA_Q_RANK, GROUP_WIDTH)
    ukv = w_ukv.reshape(MLA_KV_RANK, HEADS, MLA_NOPE_DIM + MLA_V_DIM)
    wuk_p = jnp.pad(ukv[:, :, :MLA_NOPE_DIM],
                    ((0, 0), (0, 0), (0, HEAD_LANES - MLA_NOPE_DIM))).reshape(MLA_KV_RANK, GROUP_WIDTH)
    wuv_p = ukv[:, :, MLA_NOPE_DIM:].reshape(MLA_KV_RANK, GROUP_WIDTH)
    return w_in_p, wuq_p.astype(BF16), wuk_p.astype(BF16), wuv_p.astype(BF16)


def kernel(x, c, positions, w_ada, b_ada, ffn1_norm, ffn1_w_gate, ffn1_w_up, ffn1_w_down, mix_norm, w_in, diff_lambda_q1, diff_lambda_k1, diff_lambda_q2, diff_lambda_k2, diff_subln, mla_q_norm, mla_w_uq, mla_kv_norm, mla_w_ukv, mla_out_norm, w_out, ffn2_norm, ffn2_w_gate, ffn2_w_up, ffn2_w_down, final_norm):
    batch = x.shape[0]
    assert w_ada.shape[0] == 1, "single-layer model"
    mod3 = _adaln_mod(c, w_ada[0], b_ada[0]).reshape(batch, N_MOD, D_MODEL)
    tables = _rope_tables(positions)

    h1 = _ffn1(x, mod3, ffn1_norm, ffn1_w_gate[0].astype(BF16), ffn1_w_up[0].astype(BF16),
               ffn1_w_down[0].astype(BF16))

    w_in_p, wuq_p, wuk_p, wuv_p = _layout_weights(w_in[0], mla_w_uq[0], mla_w_ukv[0])
    dq, dk, dv, mq, mk, mv = _in_proj(h1, mod3, mix_norm, tables, w_in_p,
                                      mla_q_norm, mla_kv_norm, wuq_p, wuk_p, wuv_p)

    od = _attention(_diff_attn_kernel, 1, dq, dk, dv,
                    [diff_lambda_q1, diff_lambda_k1, diff_lambda_q2, diff_lambda_k2, diff_subln],
                    "diff_attn")
    om = _attention(_mla_attn_kernel, SCORE_GROUPS, mq, mk, mv, [], "mla_attn")

    w_out_b = w_out[0].astype(BF16)
    return _out_ffn2(od, om, h1, mod3, mla_out_norm, w_out_b[:GROUP_WIDTH], w_out_b[GROUP_WIDTH:],
                     ffn2_norm, ffn2_w_gate[0].astype(BF16), ffn2_w_up[0].astype(BF16),
                     ffn2_w_down[0].astype(BF16), final_norm.reshape(1, D_MODEL))
```

```python
import functools
import math

import jax
import jax.numpy as jnp
from jax import lax
from jax.experimental import pallas as pl
from jax.experimental.pallas import tpu as pltpu

F32 = jnp.float32
BF16 = jnp.bfloat16

D_MODEL = 1024
D_FF = 2816
N_MOD = 9
HEADS = 4
HEAD_LANES = 128
HALF_LANES = HEAD_LANES // 2
DIFF_HEAD_DIM = 64
MLA_NOPE_DIM = 64
MLA_ROPE_DIM = 32
MLA_QK_DIM = MLA_NOPE_DIM + MLA_ROPE_DIM
MLA_V_DIM = 128
MLA_Q_RANK = 384
MLA_KV_RANK = 256
GROUP_WIDTH = HEADS * HEAD_LANES
ROPE_THETA = 10000.0
NORM_EPS = 1e-6
LAMBDA_INIT = 0.8 - 0.6 * math.exp(-0.3 * 0)
MASK_VALUE = -1e30
LOG2_E = math.log2(math.e)

TOKEN_TILE = 512
FF_CHUNK = 1408
ATTN_TILE = 256
VMEM_LIMIT_BYTES = 56 * 1024 * 1024


def _compiler_params(n_axes):
    return pltpu.CompilerParams(
        dimension_semantics=("parallel",) * n_axes,
        vmem_limit_bytes=VMEM_LIMIT_BYTES)


def _resident(shape):
    return pl.BlockSpec(shape, lambda *_: (0,) * len(shape),
                        pipeline_mode=pl.Buffered(1))


def _rmsnorm(x, g):
    return x * lax.rsqrt(jnp.mean(x * x, axis=-1, keepdims=True) + NORM_EPS) * g


def _bdot(a, b):
    return jnp.dot(a, b, preferred_element_type=F32)


def _mod_kernel(c_ref, w_ref, b_ref, o_ref):
    c_act = jax.nn.silu(c_ref[...]).astype(BF16)
    o_ref[...] = _bdot(c_act, w_ref[...].astype(BF16)) + b_ref[...]


def _adaln_mod(c, w_ada, b_ada):
    batch = c.shape[0]
    n_cols = w_ada.shape[1]
    col_tile = D_MODEL
    return pl.pallas_call(
        _mod_kernel,
        grid=(n_cols // col_tile,),
        in_specs=[pl.BlockSpec((batch, D_MODEL), lambda j: (0, 0)),
                  pl.BlockSpec((D_MODEL, col_tile), lambda j: (0, j)),
                  pl.BlockSpec((1, col_tile), lambda j: (0, j))],
        out_specs=pl.BlockSpec((batch, col_tile), lambda j: (0, j)),
        out_shape=jax.ShapeDtypeStruct((batch, n_cols), F32),
        compiler_params=_compiler_params(1),
        name="adaln_mod",
    )(c, w_ada, b_ada.reshape(1, n_cols))


def _rope_table_kernel(pos_ref, fd_ref, fr_ref, cosd_ref, sind_ref, cosr_ref, sinr_ref):
    pos = pos_ref[0].astype(F32)
    ang_d = fd_ref[...] * pos
    ang_r = fr_ref[...] * pos
    cd, sd = jnp.cos(ang_d), jnp.sin(ang_d)
    cr, sr = jnp.cos(ang_r), jnp.sin(ang_r)
    seq = pos.shape[1]
    rest = HALF_LANES - MLA_ROPE_DIM // 2
    ones = jnp.ones((rest, seq), F32)
    zeros = jnp.zeros((rest, seq), F32)
    cosd_ref[0] = jnp.concatenate([cd, cd, cd, cd], axis=0).T
    sind_ref[0] = jnp.concatenate([-sd, -sd, sd, sd], axis=0).T
    cosr_ref[0] = jnp.concatenate([cr, ones, cr, ones], axis=0).T
    sinr_ref[0] = jnp.concatenate([-sr, zeros, sr, zeros], axis=0).T


def _rope_tables(positions):
    batch, seq = positions.shape
    inv_d = ROPE_THETA ** (-jnp.arange(0, DIFF_HEAD_DIM, 2, dtype=F32) / DIFF_HEAD_DIM)
    inv_r = ROPE_THETA ** (-jnp.arange(0, MLA_ROPE_DIM, 2, dtype=F32) / MLA_ROPE_DIM)
    table = jax.ShapeDtypeStruct((batch, seq, HEAD_LANES), F32)
    table_spec = pl.BlockSpec((1, seq, HEAD_LANES), lambda b: (b, 0, 0))
    return pl.pallas_call(
        _rope_table_kernel,
        grid=(batch,),
        in_specs=[pl.BlockSpec((1, 1, seq), lambda b: (b, 0, 0)),
                  pl.BlockSpec((DIFF_HEAD_DIM // 2, 1), lambda b: (0, 0)),
                  pl.BlockSpec((MLA_ROPE_DIM // 2, 1), lambda b: (0, 0))],
        out_specs=[table_spec] * 4,
        out_shape=[table] * 4,
        compiler_params=_compiler_params(1),
        name="rope_tables",
    )(positions.reshape(batch, 1, seq), inv_d.reshape(-1, 1), inv_r.reshape(-1, 1))


def _ffn_halfstep(h, shift, scale, gate, norm_w, wg_ref, wu_ref, wd_ref):
    u = (_rmsnorm(h, norm_w) * (1 + scale) + shift).astype(BF16)
    y = None
    for c0 in range(0, D_FF, FF_CHUNK):
        g = _bdot(u, wg_ref[:, c0:c0 + FF_CHUNK])
        up = _bdot(u, wu_ref[:, c0:c0 + FF_CHUNK])
        a = (jax.nn.silu(g) * up).astype(BF16)
        part = _bdot(a, wd_ref[c0:c0 + FF_CHUNK, :])
        y = part if y is None else y + part
    return h + (0.5 * gate) * y


def _ffn1_kernel(x_ref, mod_ref, nw_ref, wg_ref, wu_ref, wd_ref, o_ref):
    mod = mod_ref[0]
    o_ref[0] = _ffn_halfstep(x_ref[0], mod[0:1], mod[1:2], mod[2:3], nw_ref[...],
                             wg_ref, wu_ref, wd_ref)


def _ffn1(x, mod3, norm_w, wg, wu, wd):
    batch, seq, _ = x.shape
    tile = pl.BlockSpec((1, TOKEN_TILE, D_MODEL), lambda b, i: (b, i, 0))
    return pl.pallas_call(
        _ffn1_kernel,
        grid=(batch, seq // TOKEN_TILE),
        in_specs=[tile,
                  pl.BlockSpec((1, N_MOD, D_MODEL), lambda b, i: (b, 0, 0)),
                  _resident((1, D_MODEL)),
                  _resident((D_MODEL, D_FF)), _resident((D_MODEL, D_FF)),
                  _resident((D_FF, D_MODEL))],
        out_specs=tile,
        out_shape=jax.ShapeDtypeStruct(x.shape, F32),
        compiler_params=_compiler_params(2),
        name="ffn1",
    )(x, mod3, norm_w, wg, wu, wd)


_CQ0 = 0
_CKV0 = _CQ0 + MLA_Q_RANK
_KR0 = _CKV0 + MLA_KV_RANK
LATENT_COLS = _KR0 + HEAD_LANES
_DQ0 = 0
_DK0 = GROUP_WIDTH
_DV0 = 2 * GROUP_WIDTH
IN_COLS_PADDED = LATENT_COLS + 3 * GROUP_WIDTH


def _rope(x, cos, signed_sin):
    return x * cos + pltpu.roll(x, HALF_LANES, 1) * signed_sin


def _inproj_kernel(h_ref, mod_ref, nw_ref, cosd_ref, sind_ref, cosr_ref, sinr_ref,
                   win_ref, qn_ref, kvn_ref, wuq_ref, wuk_ref, wuv_ref,
                   dq_ref, dk_ref, dv_ref, mq_ref, mk_ref, mv_ref):
    mod = mod_ref[0]
    u = (_rmsnorm(h_ref[0], nw_ref[...]) * (1 + mod[4:5]) + mod[3:4]).astype(BF16)
    latent = _bdot(u, win_ref[:, 0:LATENT_COLS])
    proj = _bdot(u, win_ref[:, LATENT_COLS:])

    cosd, sind = cosd_ref[0], sind_ref[0]
    cosr, sinr = cosr_ref[0], sinr_ref[0]
    cq = _rmsnorm(latent[:, _CQ0:_CQ0 + MLA_Q_RANK], qn_ref[...]).astype(BF16)
    ckv = _rmsnorm(latent[:, _CKV0:_CKV0 + MLA_KV_RANK], kvn_ref[...]).astype(BF16)
    mq = _bdot(cq, wuq_ref[...])
    mk = _bdot(ckv, wuk_ref[...])
    mv = _bdot(ckv, wuv_ref[...])
    kr = _rope(latent[:, _KR0:_KR0 + HEAD_LANES], cosr, sinr)

    diff_scale = DIFF_HEAD_DIM ** -0.5 * LOG2_E
    mla_scale = MLA_QK_DIM ** -0.5 * LOG2_E
    for hd in range(HEADS):
        c0 = hd * HEAD_LANES
        cols = slice(c0, c0 + HEAD_LANES)
        dq_ref[0, :, cols] = (_rope(proj[:, _DQ0 + c0:_DQ0 + c0 + HEAD_LANES], cosd, sind) * diff_scale).astype(BF16)
        dk_ref[0, :, cols] = _rope(proj[:, _DK0 + c0:_DK0 + c0 + HEAD_LANES], cosd, sind).astype(BF16)
        mq_ref[0, :, cols] = (_rope(mq[:, cols], cosr, sinr) * mla_scale).astype(BF16)
        mk_ref[0, :, cols] = (mk[:, cols] + kr).astype(BF16)
    dv_ref[0] = proj[:, _DV0:_DV0 + GROUP_WIDTH].astype(BF16)
    mv_ref[0] = mv.astype(BF16)


def _in_proj(h1, mod3, norm_w, tables, w_in_p, q_norm, kv_norm, wuq_p, wuk_p, wuv_p):
    batch, seq, _ = h1.shape
    idx = lambda b, i: (b, i, 0)
    group = jax.ShapeDtypeStruct((batch, seq, GROUP_WIDTH), BF16)
    group_spec = pl.BlockSpec((1, TOKEN_TILE, GROUP_WIDTH), idx)
    table_spec = pl.BlockSpec((1, TOKEN_TILE, HEAD_LANES), idx)
    return pl.pallas_call(
        _inproj_kernel,
        grid=(batch, seq // TOKEN_TILE),
        in_specs=[pl.BlockSpec((1, TOKEN_TILE, D_MODEL), idx),
                  pl.BlockSpec((1, N_MOD, D_MODEL), lambda b, i: (b, 0, 0)),
                  _resident((1, D_MODEL)),
                  table_spec, table_spec, table_spec, table_spec,
                  _resident((D_MODEL, IN_COLS_PADDED)),
                  _resident((1, MLA_Q_RANK)), _resident((1, MLA_KV_RANK)),
                  _resident((MLA_Q_RANK, GROUP_WIDTH)),
                  _resident((MLA_KV_RANK, GROUP_WIDTH)),
                  _resident((MLA_KV_RANK, GROUP_WIDTH))],
        out_specs=[group_spec] * 6,
        out_shape=[group] * 6,
        compiler_params=_compiler_params(2),
        name="in_proj",
    )(h1, mod3, norm_w, *tables, w_in_p, q_norm, kv_norm, wuq_p, wuk_p, wuv_p)


SCORE_GROUPS = 2
SCORE_WIDTH = SCORE_GROUPS * ATTN_TILE
ONES_ROWS = 16


def _causal_attention(q_ref, k_ref, v_ref, vt_ref, s2_ref, p2_ref, make_q_t, value_head, emit):
    seq = q_ref.shape[1]
    n_blocks = seq // ATTN_TILE
    for hd in range(vt_ref.shape[0]):
        vt_ref[hd, 0:HEAD_LANES, :] = v_ref[0, :, hd * HEAD_LANES:(hd + 1) * HEAD_LANES].astype(F32).T.astype(BF16)
        vt_ref[hd, HEAD_LANES:, :] = jnp.ones((ONES_ROWS, seq), BF16)
    key = lax.broadcasted_iota(jnp.int32, (ATTN_TILE, SCORE_WIDTH), 0)
    query = lax.broadcasted_iota(jnp.int32, (ATTN_TILE, SCORE_WIDTH), 1) % ATTN_TILE
    causal = key <= query

    def scores(i):
        q0, kend = i * ATTN_TILE, (i + 1) * ATTN_TILE
        s_ref = s2_ref.at[i % 2]
        q_t = make_q_t(q_ref[0, q0:kend, :].astype(F32))
        s = _bdot(k_ref[0, 0:kend, :], q_t)
        if i > 0:
            s_ref[0:q0, :] = s[0:q0]
        s_ref[q0:kend, :] = jnp.where(causal, s[q0:kend], MASK_VALUE)
        return jnp.max(s_ref[0:kend, :], axis=0, keepdims=True)

    def finish(i, m):
        kend = (i + 1) * ATTN_TILE
        s_ref, p_ref = s2_ref.at[i % 2], p2_ref.at[i % 2]
        p_ref[0:kend, :] = jnp.exp2((s_ref[0:kend, :] - m).astype(BF16))
        outs = []
        for g in range(SCORE_GROUPS):
            lanes = slice(g * ATTN_TILE, (g + 1) * ATTN_TILE)
            o_ext = _bdot(vt_ref[value_head[g], :, 0:kend], p_ref[0:kend, lanes])
            outs.append(o_ext[0:HEAD_LANES] / o_ext[HEAD_LANES:HEAD_LANES + 1])
        emit(i, outs)

    m = scores(0)
    for i in range(n_blocks):
        m_next = scores(i + 1) if i + 1 < n_blocks else None
        finish(i, m)
        m = m_next


def _diff_attn_kernel(q_ref, k_ref, v_ref, lq1_ref, lk1_ref, lq2_ref, lk2_ref, subln_ref, o_ref,
                      vt_ref, s2_ref, p2_ref):
    lane = lax.broadcasted_iota(jnp.int32, (ATTN_TILE, HEAD_LANES), 1)
    first_map = (lane % HALF_LANES) < DIFF_HEAD_DIM // 2
    lam =(jnp.exp(jnp.sum(lq1_ref[...] * lk1_ref[...], keepdims=True))
           - jnp.exp(jnp.sum(lq2_ref[...] * lk2_ref[...], keepdims=True)) + LAMBDA_INIT)

    def make_q_t(q):
        return jnp.concatenate([jnp.where(first_map, q, 0.0).T, jnp.where(first_map, 0.0, q).T],
                               axis=1).astype(BF16)

    def emit(i, outs):
        od = (outs[0] - lam * outs[1]).T
        o_ref[0, i * ATTN_TILE:(i + 1) * ATTN_TILE, :] = (
            _rmsnorm(od, subln_ref[...]) * (1 - LAMBDA_INIT)).astype(BF16)

    _causal_attention(q_ref, k_ref, v_ref, vt_ref, s2_ref, p2_ref, make_q_t, [0, 0], emit)


def _mla_attn_kernel(q_ref, k_ref, v_ref, o_ref, vt_ref, s2_ref, p2_ref):
    lane = lax.broadcasted_iota(jnp.int32, (ATTN_TILE, SCORE_GROUPS * HEAD_LANES), 1)
    head_rows = [slice(g * HEAD_LANES, (g + 1) * HEAD_LANES) for g in range(SCORE_GROUPS)]

    def make_q_t(q):
        return jnp.concatenate([jnp.where(lane // HEAD_LANES == g, q, 0.0).T for g in range(SCORE_GROUPS)],
                               axis=1).astype(BF16)

    def emit(i, outs):
        for g in range(SCORE_GROUPS):
            o_ref[0, i * ATTN_TILE:(i + 1) * ATTN_TILE, head_rows[g]] = outs[g].T.astype(BF16)

    _causal_attention(q_ref, k_ref, v_ref, vt_ref, s2_ref, p2_ref, make_q_t, list(range(SCORE_GROUPS)), emit)


def _attention(kernel, heads_per_step, q, k, v, extra, name):
    batch, seq, _ = q.shape
    lanes = heads_per_step * HEAD_LANES
    spec = pl.BlockSpec((1, seq, lanes), lambda b, h: (b, 0, h))
    extra_specs = [pl.BlockSpec(e.shape, lambda b, h: (0, 0)) for e in extra]
    return pl.pallas_call(
        kernel,
        grid=(batch, HEADS // heads_per_step),
        in_specs=[spec, spec, spec] + extra_specs,
        out_specs=spec,
        out_shape=jax.ShapeDtypeStruct((batch, seq, GROUP_WIDTH), BF16),
        scratch_shapes=[pltpu.VMEM((heads_per_step, HEAD_LANES + ONES_ROWS, seq), BF16),
                        pltpu.VMEM((2, seq, SCORE_WIDTH), F32),
                        pltpu.VMEM((2, seq, SCORE_WIDTH), BF16)],
        compiler_params=_compiler_params(2),
        name=name,
    )(q, k, v, *extra)


def _out_ffn2_kernel(od_ref, om_ref, h_ref, mod_ref, onorm_ref, woa_ref, wob_ref,
                     nw_ref, wg_ref, wu_ref, wd_ref, fnorm_ref, o_ref):
    mod = mod_ref[0]
    om = _rmsnorm(om_ref[0].astype(F32), onorm_ref[...]).astype(BF16)
    mixed = _bdot(od_ref[0], woa_ref[...]) + _bdot(om, wob_ref[...])
    h2 = h_ref[0] + mod[5:6] * mixed
    h3 = _ffn_halfstep(h2, mod[6:7], mod[7:8], mod[8:9], nw_ref[...], wg_ref, wu_ref, wd_ref)
    o_ref[0] = _rmsnorm(h3, fnorm_ref[...])


def _out_ffn2(od, om, h1, mod3, out_norm, wo_a, wo_b, norm_w, wg, wu, wd, final_norm):
    batch, seq, _ = h1.shape
    idx = lambda b, i: (b, i, 0)
    tile = pl.BlockSpec((1, TOKEN_TILE, D_MODEL), idx)
    group_spec = pl.BlockSpec((1, TOKEN_TILE, GROUP_WIDTH), idx)
    return pl.pallas_call(
        _out_ffn2_kernel,
        grid=(batch, seq // TOKEN_TILE),
        in_specs=[group_spec, group_spec, tile,
                  pl.BlockSpec((1, N_MOD, D_MODEL), lambda b, i: (b, 0, 0)),
                  _resident((1, GROUP_WIDTH)),
                  _resident((GROUP_WIDTH, D_MODEL)), _resident((GROUP_WIDTH, D_MODEL)),
                  _resident((1, D_MODEL)),
                  _resident((D_MODEL, D_FF)), _resident((D_MODEL, D_FF)),
                  _resident((D_FF, D_MODEL)),
                  _resident((1, D_MODEL))],
        out_specs=tile,
        out_shape=jax.ShapeDtypeStruct(h1.shape, F32),
        compiler_params=_compiler_params(2),
        name="out_ffn2",
    )(od, om, h1, mod3, out_norm, wo_a, wo_b, norm_w, wg, wu, wd, final_norm)


def _diff_head_layout(w):
    rows = w.shape[0]
    half = DIFF_HEAD_DIM // 2
    return w.reshape(rows, HEADS, 2, 2, half).transpose(0, 1, 3, 2, 4).reshape(rows, GROUP_WIDTH)


def _mla_head_layout(nope, rope):
    ref = nope if nope is not None else rope
    rows, n = ref.shape[0], ref.shape[1]
    half = MLA_ROPE_DIM // 2
    low = HALF_LANES - half
    if nope is None:
        nope = jnp.zeros((rows, n, MLA_NOPE_DIM), ref.dtype)
    if rope is None:
        rope = jnp.zeros((rows, n, MLA_ROPE_DIM), ref.dtype)
    pad = jnp.zeros((rows, n, HEAD_LANES - MLA_QK_DIM), ref.dtype)
    return jnp.concatenate([rope[:, :, :half], nope[:, :, :low], rope[:, :, half:], nope[:, :, low:], pad],
                           axis=2).reshape(rows, n * HEAD_LANES)


def _layout_weights(w_in, w_uq, w_ukv):
    d = w_in.shape[0]
    cq0 = 3 * GROUP_WIDTH
    kr0 = cq0 + MLA_Q_RANK + MLA_KV_RANK
    kr_block = _mla_head_layout(None, w_in[:, kr0:].reshape(d, 1, MLA_ROPE_DIM))
    w_in_p = jnp.concatenate(
        [w_in[:, cq0:kr0], kr_block,
         _diff_head_layout(w_in[:, 0:GROUP_WIDTH]), _diff_head_layout(w_in[:, GROUP_WIDTH:2 * GROUP_WIDTH]),
         w_in[:, 2 * GROUP_WIDTH:cq0]], axis=1).astype(BF16)
    uq = w_uq.reshape(MLA_Q_RANK, HEADS, MLA_QK_DIM)
    wuq_p = _mla_head_layout(uq[:, :, :MLA_NOPE_DIM], uq[:, :, MLA_NOPE_DIM:])
    ukv = w_ukv.reshape(MLA_KV_RANK, HEADS, MLA_NOPE_DIM + MLA_V_DIM)
    wuk_p = _mla_head_layout(ukv[:, :, :MLA_NOPE_DIM], None)
    wuv_p = ukv[:, :, MLA_NOPE_DIM:].reshape(MLA_KV_RANK, GROUP_WIDTH)
    return w_in_p, wuq_p.astype(BF16), wuk_p.astype(BF16), wuv_p.astype(BF16)


def kernel(x, c, positions, w_ada, b_ada, ffn1_norm, ffn1_w_gate, ffn1_w_up, ffn1_w_down, mix_norm, w_in, diff_lambda_q1, diff_lambda_k1, diff_lambda_q2, diff_lambda_k2, diff_subln, mla_q_norm, mla_w_uq, mla_kv_norm, mla_w_ukv, mla_out_norm, w_out, ffn2_norm, ffn2_w_gate, ffn2_w_up, ffn2_w_down, final_norm):
    batch = x.shape[0]
    assert w_ada.shape[0] == 1, "single-layer model"
    mod3 = _adaln_mod(c, w_ada[0], b_ada[0]).reshape(batch, N_MOD, D_MODEL)
    tables = _rope_tables(positions)

    h1 = _ffn1(x, mod3, ffn1_norm, ffn1_w_gate[0].astype(BF16), ffn1_w_up[0].astype(BF16),
               ffn1_w_down[0].astype(BF16))

    w_in_p, wuq_p, wuk_p, wuv_p = _layout_weights(w_in[0], mla_w_uq[0], mla_w_ukv[0])
    dq, dk, dv, mq, mk, mv = _in_proj(h1, mod3, mix_norm, tables, w_in_p,
                                      mla_q_norm, mla_kv_norm, wuq_p, wuk_p, wuv_p)

    od = _attention(_diff_attn_kernel, 1, dq, dk, dv,
                    [diff_lambda_q1, diff_lambda_k1, diff_lambda_q2, diff_lambda_k2, diff_subln],
                    "diff_attn")
    om = _attention(_mla_attn_kernel, SCORE_GROUPS, mq, mk, mv, [], "mla_attn")

    w_out_b = w_out[0].astype(BF16)
    return _out_ffn2(od, om, h1, mod3, mla_out_norm, w_out_b[:GROUP_WIDTH], w_out_b[GROUP_WIDTH:],
                     ffn2_norm, ffn2_w_gate[0].astype(BF16), ffn2_w_up[0].astype(BF16),
                     ffn2_w_down[0].astype(BF16), final_norm.reshape(1, D_MODEL))
```

```python
import functools
import math

import jax
import jax.numpy as jnp
from jax import lax
from jax.experimental import pallas as pl
from jax.experimental.pallas import tpu as pltpu

F32 = jnp.float32
BF16 = jnp.bfloat16

D_MODEL = 1024
D_FF = 2816
N_MOD = 9
HEADS = 4
HEAD_LANES = 128
HALF_LANES = HEAD_LANES // 2
DIFF_HEAD_DIM = 64
MLA_NOPE_DIM = 64
MLA_ROPE_DIM = 32
MLA_QK_DIM = MLA_NOPE_DIM + MLA_ROPE_DIM
MLA_V_DIM = 128
MLA_Q_RANK = 384
MLA_KV_RANK = 256
GROUP_WIDTH = HEADS * HEAD_LANES
ROPE_THETA = 10000.0
NORM_EPS = 1e-6
LAMBDA_INIT = 0.8 - 0.6 * math.exp(-0.3 * 0)
MASK_VALUE = -1e30
LOG2_E = math.log2(math.e)

TOKEN_TILE = 512
SUB_TILE = 256
MXU_TILE = 256
FF_SPLITS = (0, 6 * MXU_TILE, D_FF)
ATTN_TILE = 256
VMEM_LIMIT_BYTES = 56 * 1024 * 1024


def _compiler_params(n_axes):
    return pltpu.CompilerParams(
        dimension_semantics=("parallel",) * n_axes,
        vmem_limit_bytes=VMEM_LIMIT_BYTES)


def _resident(shape):
    return pl.BlockSpec(shape, lambda *_: (0,) * len(shape),
                        pipeline_mode=pl.Buffered(1))


def _rmsnorm(x, g):
    return x * lax.rsqrt(jnp.mean(x * x, axis=-1, keepdims=True) + NORM_EPS) * g


def _bdot(a, b):
    return jnp.dot(a, b, preferred_element_type=F32)


def _mod_kernel(c_ref, w_ref, b_ref, o_ref):
    c_act = jax.nn.silu(c_ref[...]).astype(BF16)
    o_ref[...] = _bdot(c_act, w_ref[...].astype(BF16)) + b_ref[...]


def _adaln_mod(c, w_ada, b_ada):
    batch = c.shape[0]
    n_cols = w_ada.shape[1]
    col_tile = D_MODEL
    return pl.pallas_call(
        _mod_kernel,
        grid=(n_cols // col_tile,),
        in_specs=[pl.BlockSpec((batch, D_MODEL), lambda j: (0, 0)),
                  pl.BlockSpec((D_MODEL, col_tile), lambda j: (0, j)),
                  pl.BlockSpec((1, col_tile), lambda j: (0, j))],
        out_specs=pl.BlockSpec((batch, col_tile), lambda j: (0, j)),
        out_shape=jax.ShapeDtypeStruct((batch, n_cols), F32),
        compiler_params=_compiler_params(1),
        name="adaln_mod",
    )(c, w_ada, b_ada.reshape(1, n_cols))


def _rope_table_kernel(pos_ref, fd_ref, fr_ref, cosd_ref, sind_ref, cosr_ref, sinr_ref):
    pos = pos_ref[0].astype(F32)
    ang_d = fd_ref[...] * pos
    ang_r = fr_ref[...] * pos
    cd, sd = jnp.cos(ang_d), jnp.sin(ang_d)
    cr, sr = jnp.cos(ang_r), jnp.sin(ang_r)
    seq = pos.shape[1]
    rest = HALF_LANES - MLA_ROPE_DIM // 2
    ones = jnp.ones((rest, seq), F32)
    zeros = jnp.zeros((rest, seq), F32)
    cosd_ref[0] = jnp.concatenate([cd, cd, cd, cd], axis=0).T
    sind_ref[0] = jnp.concatenate([-sd, -sd, sd, sd], axis=0).T
    cosr_ref[0] = jnp.concatenate([cr, ones, cr, ones], axis=0).T
    sinr_ref[0] = jnp.concatenate([-sr, zeros, sr, zeros], axis=0).T


def _rope_tables(positions):
    batch, seq = positions.shape
    inv_d = ROPE_THETA ** (-jnp.arange(0, DIFF_HEAD_DIM, 2, dtype=F32) / DIFF_HEAD_DIM)
    inv_r = ROPE_THETA ** (-jnp.arange(0, MLA_ROPE_DIM, 2, dtype=F32) / MLA_ROPE_DIM)
    table = jax.ShapeDtypeStruct((batch, seq, HEAD_LANES), F32)
    table_spec = pl.BlockSpec((1, seq, HEAD_LANES), lambda b: (b, 0, 0))
    return pl.pallas_call(
        _rope_table_kernel,
        grid=(batch,),
        in_specs=[pl.BlockSpec((1, 1, seq), lambda b: (b, 0, 0)),
                  pl.BlockSpec((DIFF_HEAD_DIM // 2, 1), lambda b: (0, 0)),
                  pl.BlockSpec((MLA_ROPE_DIM // 2, 1), lambda b: (0, 0))],
        out_specs=[table_spec] * 4,
        out_shape=[table] * 4,
        compiler_params=_compiler_params(1),
        name="rope_tables",
    )(positions.reshape(batch, 1, seq), inv_d.reshape(-1, 1), inv_r.reshape(-1, 1))


def _ffn_input(h, shift, scale, norm_w):
    return (_rmsnorm(h, norm_w) * (1 + scale) + shift).astype(BF16)


def _ffn_halfstep(h, u, gate, wg_ref, wu_ref, wd_ref):
    y = None
    for c0, c1 in zip(FF_SPLITS[:-1], FF_SPLITS[1:]):
        g = _bdot(u, wg_ref[:, c0:c1])
        up = _bdot(u, wu_ref[:, c0:c1])
        a = (jax.nn.silu(g) * up).astype(BF16)
        part = _bdot(a, wd_ref[c0:c1, :])
        y = part if y is None else y + part
    return h + (0.5 * gate) * y


def _ffn1_kernel(x_ref, mod_ref, nw_ref, wg_ref, wu_ref, wd_ref, o_ref):
    mod = mod_ref[0]
    for r0 in range(0, TOKEN_TILE, SUB_TILE):
        rows = slice(r0, r0 + SUB_TILE)
        x = x_ref[0, rows, :]
        u = _ffn_input(x, mod[0:1], mod[1:2], nw_ref[...])
        o_ref[0, rows, :] = _ffn_halfstep(x, u, mod[2:3], wg_ref, wu_ref, wd_ref)


def _ffn1(x, mod3, norm_w, wg, wu, wd):
    batch, seq, _ = x.shape
    tile = pl.BlockSpec((1, TOKEN_TILE, D_MODEL), lambda b, i: (b, i, 0))
    return pl.pallas_call(
        _ffn1_kernel,
        grid=(batch, seq // TOKEN_TILE),
        in_specs=[tile,
                  pl.BlockSpec((1, N_MOD, D_MODEL), lambda b, i: (b, 0, 0)),
                  _resident((1, D_MODEL)),
                  _resident((D_MODEL, D_FF)), _resident((D_MODEL, D_FF)),
                  _resident((D_FF, D_MODEL))],
        out_specs=tile,
        out_shape=jax.ShapeDtypeStruct(x.shape, F32),
        compiler_params=_compiler_params(2),
        name="ffn1",
    )(x, mod3, norm_w, wg, wu, wd)


_CQ0 = 0
_CKV0 = _CQ0 + MLA_Q_RANK
_KR0 = _CKV0 + MLA_KV_RANK
LATENT_COLS = _KR0 + HEAD_LANES
_DQ0 = 0
_DK0 = GROUP_WIDTH
_DV0 = 2 * GROUP_WIDTH
IN_COLS_PADDED = LATENT_COLS + 3 * GROUP_WIDTH


def _rope(x, cos, signed_sin):
    return x * cos + pltpu.roll(x, HALF_LANES, 1) * signed_sin


def _inproj_kernel(h_ref, mod_ref, nw_ref, cosd_ref, sind_ref, cosr_ref, sinr_ref,
                   win_ref, qn_ref, kvn_ref, wuq_ref, wuk_ref, wuv_ref,
                   dq_ref, dk_ref, dv_ref, mq_ref, mk_ref, mv_ref):
    mod = mod_ref[0]
    u = (_rmsnorm(h_ref[0], nw_ref[...]) * (1 + mod[4:5]) + mod[3:4]).astype(BF16)
    latent = _bdot(u, win_ref[:, 0:LATENT_COLS])
    proj = _bdot(u, win_ref[:, LATENT_COLS:])

    cosd, sind = cosd_ref[0], sind_ref[0]
    cosr, sinr = cosr_ref[0], sinr_ref[0]
    cq = _rmsnorm(latent[:, _CQ0:_CQ0 + MLA_Q_RANK], qn_ref[...]).astype(BF16)
    ckv = _rmsnorm(latent[:, _CKV0:_CKV0 + MLA_KV_RANK], kvn_ref[...]).astype(BF16)
    mq = _bdot(cq, wuq_ref[...])
    mk = _bdot(ckv, wuk_ref[...])
    mv = _bdot(ckv, wuv_ref[...])
    kr = _rope(latent[:, _KR0:_KR0 + HEAD_LANES], cosr, sinr)

    diff_scale = DIFF_HEAD_DIM ** -0.5 * LOG2_E
    mla_scale = MLA_QK_DIM ** -0.5 * LOG2_E
    for hd in range(HEADS):
        c0 = hd * HEAD_LANES
        cols = slice(c0, c0 + HEAD_LANES)
        dq_ref[0, :, cols] = (_rope(proj[:, _DQ0 + c0:_DQ0 + c0 + HEAD_LANES], cosd, sind) * diff_scale).astype(BF16)
        dk_ref[0, :, cols] = _rope(proj[:, _DK0 + c0:_DK0 + c0 + HEAD_LANES], cosd, sind).astype(BF16)
        mq_ref[0, :, cols] = (_rope(mq[:, cols], cosr, sinr) * mla_scale).astype(BF16)
        mk_ref[0, :, cols] = (mk[:, cols] + kr).astype(BF16)
    dv_ref[0] = proj[:, _DV0:_DV0 + GROUP_WIDTH].astype(BF16)
    mv_ref[0] = mv.astype(BF16)


def _in_proj(h1, mod3, norm_w, tables, w_in_p, q_norm, kv_norm, wuq_p, wuk_p, wuv_p):
    batch, seq, _ = h1.shape
    idx = lambda b, i: (b, i, 0)
    group = jax.ShapeDtypeStruct((batch, seq, GROUP_WIDTH), BF16)
    group_spec = pl.BlockSpec((1, TOKEN_TILE, GROUP_WIDTH), idx)
    table_spec = pl.BlockSpec((1, TOKEN_TILE, HEAD_LANES), idx)
    return pl.pallas_call(
        _inproj_kernel,
        grid=(batch, seq // TOKEN_TILE),
        in_specs=[pl.BlockSpec((1, TOKEN_TILE, D_MODEL), idx),
                  pl.BlockSpec((1, N_MOD, D_MODEL), lambda b, i: (b, 0, 0)),
                  _resident((1, D_MODEL)),
                  table_spec, table_spec, table_spec, table_spec,
                  _resident((D_MODEL, IN_COLS_PADDED)),
                  _resident((1, MLA_Q_RANK)), _resident((1, MLA_KV_RANK)),
                  _resident((MLA_Q_RANK, GROUP_WIDTH)),
                  _resident((MLA_KV_RANK, GROUP_WIDTH)),
                  _resident((MLA_KV_RANK, GROUP_WIDTH))],
        out_specs=[group_spec] * 6,
        out_shape=[group] * 6,
        compiler_params=_compiler_params(2),
        name="in_proj",
    )(h1, mod3, norm_w, *tables, w_in_p, q_norm, kv_norm, wuq_p, wuk_p, wuv_p)


SCORE_GROUPS = 2
SCORE_WIDTH = SCORE_GROUPS * ATTN_TILE


def _causal_attention(q_ref, k_ref, v_ref, vt_ref, s2_ref, p2_ref, make_q_t, value_head, emit):
    seq = q_ref.shape[1]
    n_blocks = seq // ATTN_TILE
    for hd in range(vt_ref.shape[0]):
        vt_ref[hd] = v_ref[0, :, hd * HEAD_LANES:(hd + 1) * HEAD_LANES].astype(F32).T.astype(BF16)
    key = lax.broadcasted_iota(jnp.int32, (ATTN_TILE, SCORE_WIDTH), 0)
    query = lax.broadcasted_iota(jnp.int32, (ATTN_TILE, SCORE_WIDTH), 1) % ATTN_TILE
    causal = key <= query

    def scores(i):
        q0, kend = i * ATTN_TILE, (i + 1) * ATTN_TILE
        s_ref = s2_ref.at[i % 2]
        q_t = make_q_t(q_ref[0, q0:kend, :].astype(F32))
        s = _bdot(k_ref[0, 0:kend, :], q_t)
        if i > 0:
            s_ref[0:q0, :] = s[0:q0]
        s_ref[q0:kend, :] = jnp.where(causal, s[q0:kend], MASK_VALUE)
        return jnp.max(s_ref[0:kend, :], axis=0, keepdims=True)

    def finish(i, m):
        kend = (i + 1) * ATTN_TILE
        s_ref, p_ref = s2_ref.at[i % 2], p2_ref.at[i % 2]
        p = jnp.exp2(s_ref[0:kend, :] - m)
        l = jnp.sum(p, axis=0, keepdims=True)
        p_ref[0:kend, :] = p.astype(BF16)
        outs = []
        for g in range(SCORE_GROUPS):
            lanes = slice(g * ATTN_TILE, (g + 1) * ATTN_TILE)
            outs.append(_bdot(vt_ref[value_head[g], :, 0:kend], p_ref[0:kend, lanes]) / l[:, lanes])
        emit(i, outs)

    m = scores(0)
    for i in range(n_blocks):
        m_next = scores(i + 1) if i + 1 < n_blocks else None
        finish(i, m)
        m = m_next


def _diff_attn_kernel(q_ref, k_ref, v_ref, lq1_ref, lk1_ref, lq2_ref, lk2_ref, subln_ref, o_ref,
                      vt_ref, s2_ref, p2_ref):
    lane = lax.broadcasted_iota(jnp.int32, (ATTN_TILE, HEAD_LANES), 1)
    first_map = (lane % HALF_LANES) < DIFF_HEAD_DIM // 2
    lam =(jnp.exp(jnp.sum(lq1_ref[...] * lk1_ref[...], keepdims=True))
           - jnp.exp(jnp.sum(lq2_ref[...] * lk2_ref[...], keepdims=True)) + LAMBDA_INIT)

    def make_q_t(q):
        return jnp.concatenate([jnp.where(first_map, q, 0.0).T, jnp.where(first_map, 0.0, q).T],
                               axis=1).astype(BF16)

    def emit(i, outs):
        od = (outs[0] - lam * outs[1]).T
        o_ref[0, i * ATTN_TILE:(i + 1) * ATTN_TILE, :] = (
            _rmsnorm(od, subln_ref[...]) * (1 - LAMBDA_INIT)).astype(BF16)

    _causal_attention(q_ref, k_ref, v_ref, vt_ref, s2_ref, p2_ref, make_q_t, [0, 0], emit)


def _mla_attn_kernel(q_ref, k_ref, v_ref, o_ref, vt_ref, s2_ref, p2_ref):
    lane = lax.broadcasted_iota(jnp.int32, (ATTN_TILE, SCORE_GROUPS * HEAD_LANES), 1)
    head_rows = [slice(g * HEAD_LANES, (g + 1) * HEAD_LANES) for g in range(SCORE_GROUPS)]

    def make_q_t(q):
        return jnp.concatenate([jnp.where(lane // HEAD_LANES == g, q, 0.0).T for g in range(SCORE_GROUPS)],
                               axis=1).astype(BF16)

    def emit(i, outs):
        for g in range(SCORE_GROUPS):
            o_ref[0, i * ATTN_TILE:(i + 1) * ATTN_TILE, head_rows[g]] = outs[g].T.astype(BF16)

    _causal_attention(q_ref, k_ref, v_ref, vt_ref, s2_ref, p2_ref, make_q_t, list(range(SCORE_GROUPS)), emit)


def _attention(kernel, heads_per_step, q, k, v, extra, name):
    batch, seq, _ = q.shape
    lanes = heads_per_step * HEAD_LANES
    spec = pl.BlockSpec((1, seq, lanes), lambda b, h: (b, 0, h))
    extra_specs = [pl.BlockSpec(e.shape, lambda b, h: (0, 0)) for e in extra]
    return pl.pallas_call(
        kernel,
        grid=(batch, HEADS // heads_per_step),
        in_specs=[spec, spec, spec] + extra_specs,
        out_specs=spec,
        out_shape=jax.ShapeDtypeStruct((batch, seq, GROUP_WIDTH), BF16),
        scratch_shapes=[pltpu.VMEM((heads_per_step, HEAD_LANES, seq), BF16),
                        pltpu.VMEM((2, seq, SCORE_WIDTH), F32),
                        pltpu.VMEM((2, seq, SCORE_WIDTH), BF16)],
        compiler_params=_compiler_params(2),
        name=name,
    )(q, k, v, *extra)


def _out_ffn2_kernel(od_ref, om_ref, h_ref, mod_ref, onorm_ref, woa_ref, wob_ref,
                     nw_ref, wg_ref, wu_ref, wd_ref, fnorm_ref, o_ref):
    mod = mod_ref[0]
    sub_tiles = []
    for r0 in range(0, TOKEN_TILE, SUB_TILE):
        rows = slice(r0, r0 + SUB_TILE)
        om = _rmsnorm(om_ref[0, rows, :].astype(F32), onorm_ref[...]).astype(BF16)
        mixed = _bdot(od_ref[0, rows, :], woa_ref[...]) + _bdot(om, wob_ref[...])
        h2 = h_ref[0, rows, :] + mod[5:6] * mixed
        sub_tiles.append((rows, h2, _ffn_input(h2, mod[6:7], mod[7:8], nw_ref[...])))
    for rows, h2, u in sub_tiles:
        h3 = _ffn_halfstep(h2, u, mod[8:9], wg_ref, wu_ref, wd_ref)
        o_ref[0, rows, :] = _rmsnorm(h3, fnorm_ref[...])


def _out_ffn2(od, om, h1, mod3, out_norm, wo_a, wo_b, norm_w, wg, wu, wd, final_norm):
    batch, seq, _ = h1.shape
    idx = lambda b, i: (b, i, 0)
    tile = pl.BlockSpec((1, TOKEN_TILE, D_MODEL), idx)
    group_spec = pl.BlockSpec((1, TOKEN_TILE, GROUP_WIDTH), idx)
    return pl.pallas_call(
        _out_ffn2_kernel,
        grid=(batch, seq // TOKEN_TILE),
        in_specs=[group_spec, group_spec, tile,
                  pl.BlockSpec((1, N_MOD, D_MODEL), lambda b, i: (b, 0, 0)),
                  _resident((1, GROUP_WIDTH)),
                  _resident((GROUP_WIDTH, D_MODEL)), _resident((GROUP_WIDTH, D_MODEL)),
                  _resident((1, D_MODEL)),
                  _resident((D_MODEL, D_FF)), _resident((D_MODEL, D_FF)),
                  _resident((D_FF, D_MODEL)),
                  _resident((1, D_MODEL))],
        out_specs=tile,
        out_shape=jax.ShapeDtypeStruct(h1.shape, F32),
        compiler_params=_compiler_params(2),
        name="out_ffn2",
    )(od, om, h1, mod3, out_norm, wo_a, wo_b, norm_w, wg, wu, wd, final_norm)


def _diff_head_layout(w):
    rows = w.shape[0]
    half = DIFF_HEAD_DIM // 2
    return w.reshape(rows, HEADS, 2, 2, half).transpose(0, 1, 3, 2, 4).reshape(rows, GROUP_WIDTH)


def _mla_head_layout(nope, rope):
    ref = nope if nope is not None else rope
    rows, n = ref.shape[0], ref.shape[1]
    half = MLA_ROPE_DIM // 2
    low = HALF_LANES - half
    if nope is None:
        nope = jnp.zeros((rows, n, MLA_NOPE_DIM), ref.dtype)
    if rope is None:
        rope = jnp.zeros((rows, n, MLA_ROPE_DIM), ref.dtype)
    pad = jnp.zeros((rows, n, HEAD_LANES - MLA_QK_DIM), ref.dtype)
    return jnp.concatenate([rope[:, :, :half], nope[:, :, :low], rope[:, :, half:], nope[:, :, low:], pad],
                           axis=2).reshape(rows, n * HEAD_LANES)


def _layout_weights(w_in, w_uq, w_ukv):
    d = w_in.shape[0]
    cq0 = 3 * GROUP_WIDTH
    kr0 = cq0 + MLA_Q_RANK + MLA_KV_RANK
    kr_block = _mla_head_layout(None, w_in[:, kr0:].reshape(d, 1, MLA_ROPE_DIM))
    w_in_p = jnp.concatenate(
        [w_in[:, cq0:kr0], kr_block,
         _diff_head_layout(w_in[:, 0:GROUP_WIDTH]), _diff_head_layout(w_in[:, GROUP_WIDTH:2 * GROUP_WIDTH]),
         w_in[:, 2 * GROUP_WIDTH:cq0]], axis=1).astype(BF16)
    uq = w_uq.reshape(MLA_Q_RANK, HEADS, MLA_QK_DIM)
    wuq_p = _mla_head_layout(uq[:, :, :MLA_NOPE_DIM], uq[:, :, MLA_NOPE_DIM:])
    ukv = w_ukv.reshape(MLA_KV_RANK, HEADS, MLA_NOPE_DIM + MLA_V_DIM)
    wuk_p = _mla_head_layout(ukv[:, :, :MLA_NOPE_DIM], None)
    wuv_p = ukv[:, :, MLA_NOPE_DIM:].reshape(MLA_KV_RANK, GROUP_WIDTH)
    return w_in_p, wuq_p.astype(BF16), wuk_p.astype(BF16), wuv_p.astype(BF16)


def kernel(x, c, positions, w_ada, b_ada, ffn1_norm, ffn1_w_gate, ffn1_w_up, ffn1_w_down, mix_norm, w_in, diff_lambda_q1, diff_lambda_k1, diff_lambda_q2, diff_lambda_k2, diff_subln, mla_q_norm, mla_w_uq, mla_kv_norm, mla_w_ukv, mla_out_norm, w_out, ffn2_norm, ffn2_w_gate, ffn2_w_up, ffn2_w_down, final_norm):
    batch = x.shape[0]
    assert w_ada.shape[0] == 1, "single-layer model"
    mod3 = _adaln_mod(c, w_ada[0], b_ada[0]).reshape(batch, N_MOD, D_MODEL)
    tables = _rope_tables(positions)

    h1 = _ffn1(x, mod3, ffn1_norm, ffn1_w_gate[0].astype(BF16), ffn1_w_up[0].astype(BF16),
               ffn1_w_down[0].astype(BF16))

    w_in_p, wuq_p, wuk_p, wuv_p = _layout_weights(w_in[0], mla_w_uq[0], mla_w_ukv[0])
    dq, dk, dv, mq, mk, mv = _in_proj(h1, mod3, mix_norm, tables, w_in_p,
                                      mla_q_norm, mla_kv_norm, wuq_p, wuk_p, wuv_p)

    od = _attention(_diff_attn_kernel, 1, dq, dk, dv,
                    [diff_lambda_q1, diff_lambda_k1, diff_lambda_q2, diff_lambda_k2, diff_subln],
                    "diff_attn")
    om = _attention(_mla_attn_kernel, SCORE_GROUPS, mq, mk, mv, [], "mla_attn")

    w_out_b = w_out[0].astype(BF16)
    return _out_ffn2(od, om, h1, mod3, mla_out_norm, w_out_b[:GROUP_WIDTH], w_out_b[GROUP_WIDTH:],
                     ffn2_norm, ffn2_w_gate[0].astype(BF16), ffn2_w_up[0].astype(BF16),
                     ffn2_w_down[0].astype(BF16), final_norm.reshape(1, D_MODEL))
```
